```python
import jax, jax.numpy as jnp
from jax import lax
import numpy as np

D_MODEL = 2048
BATCH = 4
SEQ = 4096
DEPTH = 2
DEC_BATCH = 8
DEC_SEQ = 4096
PAST_LEN = 128

D_FF = 5632
N_AB_LAYERS = (DEPTH + 1) // 2
N_C_LAYERS = DEPTH // 2
MLA_HEADS = 8
Q_LORA = 512
KV_LORA = 512
QK_NOPE = 128
QK_ROPE = 64
QK_HEAD = QK_NOPE + QK_ROPE
V_HEAD = 128
MLA_WIDTH = MLA_HEADS * V_HEAD
CONV_WIDTH = D_MODEL - MLA_WIDTH
CONV_K = 3
AB_IN = Q_LORA + KV_LORA + QK_ROPE + 3 * CONV_WIDTH
AB_SPLITS = [Q_LORA, Q_LORA + KV_LORA, Q_LORA + KV_LORA + QK_ROPE,
             Q_LORA + KV_LORA + QK_ROPE + CONV_WIDTH,
             Q_LORA + KV_LORA + QK_ROPE + 2 * CONV_WIDTH]
C_HEADS = 16
C_KV_HEADS = 4
C_GROUP = C_HEADS // C_KV_HEADS
C_HEAD_DIM = 128
C_IN = (C_HEADS + 2 * C_KV_HEADS) * C_HEAD_DIM
WINDOW = 128
BLOCK = 128
ROPE_THETA = 10000.0
EPS = 1e-6
NEG = -1e30

kernel_name = "hybrid_mla_shortconv_swa_macaron_encoder"


def rms_norm(x, g):
    xf = x.astype(jnp.float32)
    y = xf * lax.rsqrt(jnp.mean(xf * xf, axis=-1, keepdims=True) + EPS)
    return (y * g.astype(jnp.float32)).astype(x.dtype)


def rope_tables(seq_len, dim):
    inv = 1.0 / (ROPE_THETA ** (jnp.arange(0, dim, 2, dtype=jnp.float32) / dim))
    ang = jnp.arange(seq_len, dtype=jnp.float32)[:, None] * inv[None, :]
    return jnp.cos(ang), jnp.sin(ang)


def apply_rope(x, cos, sin):
    half = x.shape[-1] // 2
    xf = x.astype(jnp.float32)
    x1, x2 = xf[..., :half], xf[..., half:]
    c, s = cos[:, None, :], sin[:, None, :]
    return jnp.concatenate([x1 * c - x2 * s, x2 * c + x1 * s], axis=-1).astype(x.dtype)


def swiglu(x, w_gate, w_up, w_down):
    return (jax.nn.silu(x @ w_gate) * (x @ w_up)) @ w_down


def dense_attention(q, k, v, scale):
    bsz, s_len, h, dq = q.shape
    nb = s_len // BLOCK
    q_blocks = q.reshape(bsz, nb, BLOCK, h, dq).transpose(1, 0, 2, 3, 4)

    def one_block(qb):
        s = jnp.einsum('bqhd,bkhd->bhqk', qb, k).astype(jnp.float32) * scale
        p = jax.nn.softmax(s, axis=-1).astype(v.dtype)
        return jnp.einsum('bhqk,bkhd->bqhd', p, v)

    out = lax.map(one_block, q_blocks)
    return out.transpose(1, 0, 2, 3, 4).reshape(bsz, s_len, h, v.shape[-1])


def mla_conv_mixer(h, w_in, q_a_norm, w_q_b, kv_a_norm, w_kv_b, q_norm, k_norm, conv_w, w_out):
    bsz, s_len, _ = h.shape
    z = h @ w_in
    q_a, kv_a, k_rope, gate_b, gate_c, u = jnp.split(z, AB_SPLITS, axis=-1)
    q = (rms_norm(q_a, q_a_norm) @ w_q_b).reshape(bsz, s_len, MLA_HEADS, QK_HEAD)
    kv = (rms_norm(kv_a, kv_a_norm) @ w_kv_b).reshape(bsz, s_len, MLA_HEADS, QK_NOPE + V_HEAD)
    k_nope, v = kv[..., :QK_NOPE], kv[..., QK_NOPE:]
    k_r = jnp.broadcast_to(k_rope[:, :, None, :], (bsz, s_len, MLA_HEADS, QK_ROPE))
    k = jnp.concatenate([k_nope, k_r], axis=-1)
    q = rms_norm(q, q_norm)
    k = rms_norm(k, k_norm)
    cos, sin = rope_tables(s_len, QK_ROPE)
    q = jnp.concatenate([q[..., :QK_NOPE], apply_rope(q[..., QK_NOPE:], cos, sin)], axis=-1)
    k = jnp.concatenate([k[..., :QK_NOPE], apply_rope(k[..., QK_NOPE:], cos, sin)], axis=-1)
    attn = dense_attention(q, k, v, QK_HEAD ** -0.5).reshape(bsz, s_len, MLA_WIDTH)
    cu = jnp.pad(gate_c * u, ((0, 0), (1, 1), (0, 0)))
    conv = cu[:, :-2] * conv_w[0] + cu[:, 1:-1] * conv_w[1] + cu[:, 2:] * conv_w[2]
    conv_out = gate_b * conv
    return jnp.concatenate([attn, conv_out], axis=-1) @ w_out


def window_gqa_mixer(h, w_in, q_norm, k_norm, sink, w_out):
    bsz, s_len, _ = h.shape
    nb = s_len // BLOCK
    z = h @ w_in
    q, k, v = jnp.split(z, [C_HEADS * C_HEAD_DIM, (C_HEADS + C_KV_HEADS) * C_HEAD_DIM], axis=-1)
    q = rms_norm(q.reshape(bsz, s_len, C_HEADS, C_HEAD_DIM), q_norm)
    k = rms_norm(k.reshape(bsz, s_len, C_KV_HEADS, C_HEAD_DIM), k_norm)
    v = v.reshape(bsz, s_len, C_KV_HEADS, C_HEAD_DIM)
    cos, sin = rope_tables(s_len, C_HEAD_DIM)
    q = apply_rope(q, cos, sin)
    k = apply_rope(k, cos, sin)
    qb = q.reshape(bsz, nb, BLOCK, C_KV_HEADS, C_GROUP, C_HEAD_DIM).transpose(1, 0, 2, 3, 4, 5)

    def band(t):
        tp = jnp.pad(t, ((0, 0), (BLOCK, BLOCK), (0, 0), (0, 0)))
        tp = tp.reshape(bsz, nb + 2, BLOCK, C_KV_HEADS, C_HEAD_DIM)
        tb = jnp.concatenate([tp[:, :-2], tp[:, 1:-1], tp[:, 2:]], axis=2)
        return tb.transpose(1, 0, 2, 3, 4)

    kb, vb = band(k), band(v)
    blk = jnp.arange(nb)[:, None, None] * BLOCK
    qpos = blk + jnp.arange(BLOCK)[None, :, None]
    kpos = blk - BLOCK + jnp.arange(3 * BLOCK)[None, None, :]
    valid = (jnp.abs(qpos - kpos) <= WINDOW) & (kpos >= 0) & (kpos < s_len)
    sink_f = sink.astype(jnp.float32).reshape(C_KV_HEADS, C_GROUP)

    def one_block(args):
        qblk, kblk, vblk, msk = args
        s = jnp.einsum('bqhgd,bkhd->bhgqk', qblk, kblk).astype(jnp.float32) * (C_HEAD_DIM ** -0.5)
        s = jnp.where(msk[None, None, None], s, NEG)
        sink_col = jnp.broadcast_to(sink_f[None, :, :, None, None], s.shape[:-1] + (1,))
        p = jax.nn.softmax(jnp.concatenate([s, sink_col], axis=-1), axis=-1)[..., :-1]
        return jnp.einsum('bhgqk,bkhd->bqhgd', p.astype(vblk.dtype), vblk)

    out = lax.map(one_block, (qb, kb, vb, valid))
    out = out.transpose(1, 0, 2, 3, 4, 5).reshape(bsz, s_len, C_HEADS * C_HEAD_DIM)
    return out @ w_out


def trunk(x, ffn_norm, ffn_w_gate, ffn_w_up, ffn_w_down, mix_norm,
          ab_w_in, ab_q_a_norm, ab_w_q_b, ab_kv_a_norm, ab_w_kv_b, ab_q_norm, ab_k_norm,
          ab_conv_w, ab_w_out, c_w_in, c_q_norm, c_k_norm, c_sink, c_w_out):
    for layer in range(DEPTH):
        x = x + 0.5 * swiglu(rms_norm(x, ffn_norm[layer, 0]), ffn_w_gate[layer, 0],
                             ffn_w_up[layer, 0], ffn_w_down[layer, 0])
        h = rms_norm(x, mix_norm[layer])
        i = layer // 2
        if layer % 2 == 0:
            x = x + mla_conv_mixer(h, ab_w_in[i], ab_q_a_norm[i], ab_w_q_b[i], ab_kv_a_norm[i],
                                   ab_w_kv_b[i], ab_q_norm[i], ab_k_norm[i], ab_conv_w[i], ab_w_out[i])
        else:
            x = x + window_gqa_mixer(h, c_w_in[i], c_q_norm[i], c_k_norm[i], c_sink[i], c_w_out[i])
        x = x + 0.5 * swiglu(rms_norm(x, ffn_norm[layer, 1]), ffn_w_gate[layer, 1],
                             ffn_w_up[layer, 1], ffn_w_down[layer, 1])
    return x


def setup_inputs(seed: int = 0) -> dict:
    key = jax.random.key(seed)
    ks = jax.random.split(key, 24)
    f32 = jnp.float32

    def w(k, shape, fan_in):
        return jax.random.normal(k, shape, f32) * (fan_in ** -0.5)

    def gain(k, shape):
        return 1.0 + 0.01 * jax.random.normal(k, shape, f32)

    return {
        "x_prompt": jax.random.normal(ks[0], (BATCH, SEQ, D_MODEL), f32),
        "x_sample": jax.random.normal(ks[1], (DEC_BATCH, DEC_SEQ, D_MODEL), f32),
        "ffn_norm": gain(ks[2], (DEPTH, 2, D_MODEL)),
        "ffn_w_gate": w(ks[3], (DEPTH, 2, D_MODEL, D_FF), D_MODEL),
        "ffn_w_up": w(ks[4], (DEPTH, 2, D_MODEL, D_FF), D_MODEL),
        "ffn_w_down": w(ks[5], (DEPTH, 2, D_FF, D_MODEL), D_FF),
        "mix_norm": gain(ks[6], (DEPTH, D_MODEL)),
        "ab_w_in": w(ks[7], (N_AB_LAYERS, D_MODEL, AB_IN), D_MODEL),
        "ab_q_a_norm": gain(ks[8], (N_AB_LAYERS, Q_LORA)),
        "ab_w_q_b": w(ks[9], (N_AB_LAYERS, Q_LORA, MLA_HEADS * QK_HEAD), Q_LORA),
        "ab_kv_a_norm": gain(ks[10], (N_AB_LAYERS, KV_LORA)),
        "ab_w_kv_b": w(ks[11], (N_AB_LAYERS, KV_LORA, MLA_HEADS * (QK_NOPE + V_HEAD)), KV_LORA),
        "ab_q_norm": gain(ks[12], (N_AB_LAYERS, QK_HEAD)),
        "ab_k_norm": gain(ks[13], (N_AB_LAYERS, QK_HEAD)),
        "ab_conv_w": w(ks[14], (N_AB_LAYERS, CONV_K, CONV_WIDTH), CONV_K),
        "ab_w_out": w(ks[15], (N_AB_LAYERS, D_MODEL, D_MODEL), D_MODEL),
        "c_w_in": w(ks[16], (N_C_LAYERS, D_MODEL, C_IN), D_MODEL),
        "c_q_norm": gain(ks[17], (N_C_LAYERS, C_HEAD_DIM)),
        "c_k_norm": gain(ks[18], (N_C_LAYERS, C_HEAD_DIM)),
        "c_sink": jax.random.normal(ks[19], (N_C_LAYERS, C_HEADS), f32),
        "c_w_out": w(ks[20], (N_C_LAYERS, C_HEADS * C_HEAD_DIM, D_MODEL), C_HEADS * C_HEAD_DIM),
    }


def reference(x_prompt, x_sample, ffn_norm, ffn_w_gate, ffn_w_up, ffn_w_down, mix_norm,
              ab_w_in, ab_q_a_norm, ab_w_q_b, ab_kv_a_norm, ab_w_kv_b, ab_q_norm, ab_k_norm,
              ab_conv_w, ab_w_out, c_w_in, c_q_norm, c_k_norm, c_sink, c_w_out):
    y_prompt = trunk(x_prompt, ffn_norm, ffn_w_gate, ffn_w_up, ffn_w_down, mix_norm,
                     ab_w_in, ab_q_a_norm, ab_w_q_b, ab_kv_a_norm, ab_w_kv_b, ab_q_norm, ab_k_norm,
                     ab_conv_w, ab_w_out, c_w_in, c_q_norm, c_k_norm, c_sink, c_w_out)
    y_sample = trunk(x_sample, ffn_norm, ffn_w_gate, ffn_w_up, ffn_w_down, mix_norm,
                     ab_w_in, ab_q_a_norm, ab_w_q_b, ab_kv_a_norm, ab_w_kv_b, ab_q_norm, ab_k_norm,
                     ab_conv_w, ab_w_out, c_w_in, c_q_norm, c_k_norm, c_sink, c_w_out)
    return (y_prompt, y_sample)
```

```python
import functools

import jax
import jax.numpy as jnp
from jax import lax
from jax.experimental import pallas as pl
from jax.experimental.pallas import tpu as pltpu

D_MODEL = 2048
D_FF = 5632
MLA_HEADS = 8
Q_LORA = 512
KV_LORA = 512
QK_NOPE = 128
QK_ROPE = 64
QK_HEAD = QK_NOPE + QK_ROPE
V_HEAD = 128
MLA_WIDTH = MLA_HEADS * V_HEAD
CONV_WIDTH = D_MODEL - MLA_WIDTH
C_HEADS = 16
C_KV_HEADS = 4
C_GROUP = C_HEADS // C_KV_HEADS
C_HEAD_DIM = 128
WINDOW = 128
ROPE_THETA = 10000.0
EPS = 1e-6
NEG = -1e30

LANES = 128
MLA_QK_PAD = 2 * LANES
AB_IN_PAD = Q_LORA + KV_LORA + 3 * CONV_WIDTH + LANES
VMEM_LIMIT = 56 * 1024 * 1024

F32 = jnp.float32
BF16 = jnp.bfloat16


def _params(*sem):
    return pltpu.CompilerParams(dimension_semantics=sem, vmem_limit_bytes=VMEM_LIMIT)


def _resident(shape):
    nd = len(shape)
    return pl.BlockSpec(shape, lambda *_: (0,) * nd, pipeline_mode=pl.Buffered(1))


def _rms(x, gain):
    ms = jnp.mean(x * x, axis=-1, keepdims=True)
    return x * lax.rsqrt(ms + EPS) * gain


def _ffn_body(x_ref, g_ref, wg_ref, wu_ref, wd_ref, o_ref, h_ref):
    j = pl.program_id(1)

    @pl.when(j == 0)
    def _():
        x = x_ref[...]
        h_ref[...] = _rms(x, g_ref[...]).astype(BF16)
        o_ref[...] = x

    h = h_ref[...]
    gate = jnp.dot(h, wg_ref[...], preferred_element_type=F32)
    up = jnp.dot(h, wu_ref[...], preferred_element_type=F32)
    act = (gate * jax.nn.sigmoid(gate) * up * 0.5).astype(BF16)
    o_ref[...] += jnp.dot(act, wd_ref[...], preferred_element_type=F32)


def _ffn(x, gain, wg, wu, wd, *, tm=512, tf=512):
    t, d = x.shape
    f = wg.shape[1]
    return pl.pallas_call(
        _ffn_body,
        grid=(t // tm, f // tf),
        in_specs=[
            pl.BlockSpec((tm, d), lambda i, j: (i, 0)),
            pl.BlockSpec((1, d), lambda i, j: (0, 0)),
            pl.BlockSpec((d, tf), lambda i, j: (0, j)),
            pl.BlockSpec((d, tf), lambda i, j: (0, j)),
            pl.BlockSpec((tf, d), lambda i, j: (j, 0)),
        ],
        out_specs=pl.BlockSpec((tm, d), lambda i, j: (i, 0)),
        out_shape=jax.ShapeDtypeStruct((t, d), F32),
        scratch_shapes=[pltpu.VMEM((tm, d), BF16)],
        compiler_params=_params("parallel", "arbitrary"),
        name="ffn",
    )(x, gain, wg, wu, wd)


def _mla_proj_body(x_ref, g_ref, win_ref, qag_ref, kvag_ref, wqb_ref, wkvb_ref, gq_ref, gk_ref,
                   cos_ref, sin_ref, q_ref, k_ref, v_ref, gb_ref, cu_ref):
    h = _rms(x_ref[...], g_ref[...]).astype(BF16)
    z = jnp.dot(h, win_ref[...], preferred_element_type=F32)
    c0 = Q_LORA + KV_LORA
    gb_ref[...] = z[:, c0:c0 + CONV_WIDTH].astype(BF16)
    cu_ref[...] = (z[:, c0 + CONV_WIDTH:c0 + 2 * CONV_WIDTH]
                   * z[:, c0 + 2 * CONV_WIDTH:c0 + 3 * CONV_WIDTH]).astype(BF16)
    k_rope = z[:, c0 + 3 * CONV_WIDTH:]

    qa = _rms(z[:, :Q_LORA], qag_ref[...]).astype(BF16)
    kva = _rms(z[:, Q_LORA:c0], kvag_ref[...]).astype(BF16)
    q = jnp.dot(qa, wqb_ref[...], preferred_element_type=F32)
    kv = jnp.dot(kva, wkvb_ref[...], preferred_element_type=F32)
    v_ref[...] = kv[:, MLA_HEADS * QK_NOPE:].astype(BF16)

    cos = cos_ref[...]
    sin = sin_ref[...]
    gq = gq_ref[...]
    gk = gk_ref[...]

    def rope(t):
        return t * cos + pltpu.roll(t, LANES // 2, 1) * sin

    kr_rot = rope(k_rope * gk[:, LANES:])
    kr_ss = jnp.sum(k_rope * k_rope, axis=-1, keepdims=True)
    for hd in range(MLA_HEADS):
        lo = hd * MLA_QK_PAD
        qn = q[:, lo:lo + LANES]
        qr = q[:, lo + LANES:lo + MLA_QK_PAD]
        ss = jnp.sum(qn * qn, axis=-1, keepdims=True) + jnp.sum(qr * qr, axis=-1, keepdims=True)
        r = lax.rsqrt(ss * (1.0 / QK_HEAD) + EPS)
        q_ref[:, lo:lo + LANES] = (qn * r * gq[:, :LANES]).astype(BF16)
        q_ref[:, lo + LANES:lo + MLA_QK_PAD] = rope(qr * r * gq[:, LANES:]).astype(BF16)
        kn = kv[:, hd * QK_NOPE:(hd + 1) * QK_NOPE]
        ssk = jnp.sum(kn * kn, axis=-1, keepdims=True) + kr_ss
        rk = lax.rsqrt(ssk * (1.0 / QK_HEAD) + EPS)
        k_ref[:, lo:lo + LANES] = (kn * rk * gk[:, :LANES]).astype(BF16)
        k_ref[:, lo + LANES:lo + MLA_QK_PAD] = (kr_rot * rk).astype(BF16)


def _mla_proj(x, gain, win, qag, kvag, wqb, wkvb, gq, gk, cos, sin, *, tm=256):
    b, s, d = x.shape
    tok = lambda w: pl.BlockSpec((None, tm, w), lambda bi, i: (bi, i, 0))
    bf = lambda w: jax.ShapeDtypeStruct((b, s, w), BF16)
    return pl.pallas_call(
        _mla_proj_body,
        grid=(b, s // tm),
        in_specs=[
            tok(d),
            _resident((1, d)),
            _resident(win.shape),
            _resident((1, Q_LORA)),
            _resident((1, KV_LORA)),
            _resident(wqb.shape),
            _resident(wkvb.shape),
            _resident((1, MLA_QK_PAD)),
            _resident((1, MLA_QK_PAD)),
            pl.BlockSpec((tm, LANES), lambda bi, i: (i, 0)),
            pl.BlockSpec((tm, LANES), lambda bi, i: (i, 0)),
        ],
        out_specs=[tok(MLA_HEADS * MLA_QK_PAD), tok(MLA_HEADS * MLA_QK_PAD), tok(MLA_WIDTH),
                   tok(CONV_WIDTH), tok(CONV_WIDTH)],
        out_shape=[bf(MLA_HEADS * MLA_QK_PAD), bf(MLA_HEADS * MLA_QK_PAD), bf(MLA_WIDTH),
                   bf(CONV_WIDTH), bf(CONV_WIDTH)],
        compiler_params=_params("parallel", "parallel"),
        name="mla_proj",
    )(x, gain, win, qag, kvag, wqb, wkvb, gq, gk, cos, sin)


def _mla_attn_body(q_ref, k_ref, v_ref, o_ref):
    s = lax.dot_general(q_ref[...], k_ref[...], (((1,), (1,)), ((), ())),
                        preferred_element_type=F32)
    m = jnp.max(s, axis=-1, keepdims=True)
    p = jnp.exp(s - m)
    l = jnp.sum(p, axis=-1, keepdims=True)
    o = jnp.dot(p.astype(BF16), v_ref[...], preferred_element_type=F32)
    o_ref[...] = (o / l).astype(BF16)


def _mla_attn(q, k, v, *, tq=256):
    b, s, _ = q.shape
    return pl.pallas_call(
        _mla_attn_body,
        grid=(b, MLA_HEADS, s // tq),
        in_specs=[
            pl.BlockSpec((None, tq, MLA_QK_PAD), lambda bi, h, i: (bi, i, h)),
            pl.BlockSpec((None, s, MLA_QK_PAD), lambda bi, h, i: (bi, 0, h)),
            pl.BlockSpec((None, s, V_HEAD), lambda bi, h, i: (bi, 0, h)),
        ],
        out_specs=pl.BlockSpec((None, tq, V_HEAD), lambda bi, h, i: (bi, i, h)),
        out_shape=jax.ShapeDtypeStruct((b, s, MLA_WIDTH), BF16),
        compiler_params=_params("parallel", "parallel", "arbitrary"),
        name="mla_attn",
    )(q, k, v)


CONV_HALO = 16


def _mla_out_body(attn_ref, gb_ref, cu_ref, prev_ref, next_ref, cw_ref, wo_ref, x_ref, o_ref):
    i = pl.program_id(1)
    tm = cu_ref.shape[0]
    cu = cu_ref[...].astype(F32)
    prev_row = prev_ref[...].astype(F32)[CONV_HALO - 1:CONV_HALO, :] * (i > 0).astype(F32)
    next_row = next_ref[...].astype(F32)[0:1, :] * (i < pl.num_programs(1) - 1).astype(F32)
    row = lax.broadcasted_iota(jnp.int32, (tm, 1), 0)
    before = jnp.where(row == 0, prev_row, pltpu.roll(cu, 1, 0))
    after = jnp.where(row == tm - 1, next_row, pltpu.roll(cu, tm - 1, 0))
    cw = cw_ref[...]
    conv = before * cw[0:1, :] + cu * cw[1:2, :] + after * cw[2:3, :]
    conv_out = (gb_ref[...].astype(F32) * conv).astype(BF16)
    y = jnp.dot(attn_ref[...], wo_ref[:MLA_WIDTH, :], preferred_element_type=F32)
    y += jnp.dot(conv_out, wo_ref[MLA_WIDTH:, :], preferred_element_type=F32)
    o_ref[...] = x_ref[...] + y


def _mla_out(attn, gb, cu, conv_w, wo, x, *, tm=512):
    b, s, d = x.shape
    hb = tm // CONV_HALO
    last = s // CONV_HALO - 1
    tok = lambda w: pl.BlockSpec((None, tm, w), lambda bi, i: (bi, i, 0))
    return pl.pallas_call(
        _mla_out_body,
        grid=(b, s // tm),
        in_specs=[
            tok(MLA_WIDTH), tok(CONV_WIDTH), tok(CONV_WIDTH),
            pl.BlockSpec((None, CONV_HALO, CONV_WIDTH),
                         lambda bi, i: (bi, jnp.maximum(i * hb - 1, 0), 0)),
            pl.BlockSpec((None, CONV_HALO, CONV_WIDTH),
                         lambda bi, i: (bi, jnp.minimum((i + 1) * hb, last), 0)),
            _resident(conv_w.shape),
            _resident(wo.shape),
            tok(d),
        ],
        out_specs=tok(d),
        out_shape=jax.ShapeDtypeStruct((b, s, d), F32),
        compiler_params=_params("parallel", "parallel"),
        name="mla_out",
    )(attn, gb, cu, cu, cu, conv_w, wo, x)


def _gqa_proj_body(x_ref, g_ref, win_ref, gq_ref, gk_ref, cos_ref, sin_ref, q_ref, k_ref, v_ref):
    h = _rms(x_ref[...], g_ref[...]).astype(BF16)
    z = jnp.dot(h, win_ref[...], preferred_element_type=F32)
    cos = cos_ref[...]
    sin = sin_ref[...]
    gq = gq_ref[...]
    gk = gk_ref[...]

    def norm_rope(t, gain):
        t = _rms(t, gain)
        return (t * cos + pltpu.roll(t, C_HEAD_DIM // 2, 1) * sin).astype(BF16)

    for hd in range(C_HEADS):
        lo = hd * C_HEAD_DIM
        q_ref[:, lo:lo + C_HEAD_DIM] = norm_rope(z[:, lo:lo + C_HEAD_DIM], gq)
    k0 = C_HEADS * C_HEAD_DIM
    for hd in range(C_KV_HEADS):
        lo = hd * C_HEAD_DIM
        k_ref[:, lo:lo + C_HEAD_DIM] = norm_rope(z[:, k0 + lo:k0 + lo + C_HEAD_DIM], gk)
    v_ref[...] = z[:, k0 + C_KV_HEADS * C_HEAD_DIM:].astype(BF16)


def _gqa_proj(x, gain, win, gq, gk, cos, sin, *, tm=256):
    b, s, d = x.shape
    tok = lambda w: pl.BlockSpec((None, tm, w), lambda bi, i: (bi, i, 0))
    bf = lambda w: jax.ShapeDtypeStruct((b, s, w), BF16)
    kvw = C_KV_HEADS * C_HEAD_DIM
    return pl.pallas_call(
        _gqa_proj_body,
        grid=(b, s // tm),
        in_specs=[
            tok(d),
            _resident((1, d)),
            _resident(win.shape),
            _resident((1, C_HEAD_DIM)),
            _resident((1, C_HEAD_DIM)),
            pl.BlockSpec((tm, C_HEAD_DIM), lambda bi, i: (i, 0)),
            pl.BlockSpec((tm, C_HEAD_DIM), lambda bi, i: (i, 0)),
        ],
        out_specs=[tok(C_HEADS * C_HEAD_DIM), tok(kvw), tok(kvw)],
        out_shape=[bf(C_HEADS * C_HEAD_DIM), bf(kvw), bf(kvw)],
        compiler_params=_params("parallel", "parallel"),
        name="gqa_proj",
    )(x, gain, win, gq, gk, cos, sin)


WIN_TQ = 256
WIN_KW = WIN_TQ + 2 * WINDOW


def _win_attn_body(q_ref, k_ref, v_ref, sink_ref, o_ref):
    i = pl.program_id(2)
    s_len = k_ref.shape[0]
    start = pl.multiple_of(jnp.clip(i * WIN_TQ - WINDOW, 0, s_len - WIN_KW), WINDOW)
    kk = k_ref[pl.ds(start, WIN_KW), :]
    vv = v_ref[pl.ds(start, WIN_KW), :]
    qpos = i * WIN_TQ + lax.broadcasted_iota(jnp.int32, (WIN_TQ, 1), 0)
    kpos = start + lax.broadcasted_iota(jnp.int32, (1, WIN_KW), 1)
    valid = jnp.abs(qpos - kpos) <= WINDOW
    sink = sink_ref[...]
    for g in range(C_GROUP):
        lo = g * C_HEAD_DIM
        s = lax.dot_general(q_ref[:, lo:lo + C_HEAD_DIM], kk, (((1,), (1,)), ((), ())),
                            preferred_element_type=F32)
        s = jnp.where(valid, s, NEG)
        sk = sink[g:g + 1, 0:1]
        m = jnp.maximum(jnp.max(s, axis=-1, keepdims=True), sk)
        p = jnp.exp(s - m)
        l = jnp.sum(p, axis=-1, keepdims=True) + jnp.exp(sk - m)
        o = jnp.dot(p.astype(BF16), vv, preferred_element_type=F32)
        o_ref[:, lo:lo + C_HEAD_DIM] = (o / l).astype(BF16)


def _win_attn(q, k, v, sink):
    b, s, _ = q.shape
    gw = C_GROUP * C_HEAD_DIM
    return pl.pallas_call(
        _win_attn_body,
        grid=(b, C_KV_HEADS, s // WIN_TQ),
        in_specs=[
            pl.BlockSpec((None, WIN_TQ, gw), lambda bi, h, i: (bi, i, h)),
            pl.BlockSpec((None, s, C_HEAD_DIM), lambda bi, h, i: (bi, 0, h)),
            pl.BlockSpec((None, s, C_HEAD_DIM), lambda bi, h, i: (bi, 0, h)),
            pl.BlockSpec((None, C_GROUP, LANES), lambda bi, h, i: (h, 0, 0)),
        ],
        out_specs=pl.BlockSpec((None, WIN_TQ, gw), lambda bi, h, i: (bi, i, h)),
        out_shape=jax.ShapeDtypeStruct((b, s, C_HEADS * C_HEAD_DIM), BF16),
        compiler_params=_params("parallel", "parallel", "arbitrary"),
        name="win_attn",
    )(q, k, v, sink)


def _proj_residual_body(a_ref, w_ref, x_ref, o_ref):
    o_ref[...] = x_ref[...] + jnp.dot(a_ref[...], w_ref[...], preferred_element_type=F32)


def _proj_residual(a, w, x, *, tm=512):
    b, s, d = x.shape
    tok = lambda width: pl.BlockSpec((None, tm, width), lambda bi, i: (bi, i, 0))
    return pl.pallas_call(
        _proj_residual_body,
        grid=(b, s // tm),
        in_specs=[tok(a.shape[-1]), _resident(w.shape), tok(d)],
        out_specs=tok(d),
        out_shape=jax.ShapeDtypeStruct((b, s, d), F32),
        compiler_params=_params("parallel", "parallel"),
        name="proj_residual",
    )(a, w, x)


def _rope_tables(seq_len, dim):
    inv = 1.0 / (ROPE_THETA ** (jnp.arange(0, dim, 2, dtype=F32) / dim))
    ang = jnp.arange(seq_len, dtype=F32)[:, None] * inv[None, :]
    return jnp.cos(ang), jnp.sin(ang)


def _rope_slab(x1, x2):
    z = jnp.zeros_like(x1)
    return jnp.concatenate([x1, z, x2, z], axis=-1)


def _prep_mla(w_in, w_q_b, w_kv_b, q_norm, k_norm, seq_len):
    half = QK_ROPE // 2
    c0 = Q_LORA + KV_LORA
    kr = w_in[:, c0:c0 + QK_ROPE]
    win = jnp.concatenate([w_in[:, :c0], w_in[:, c0 + QK_ROPE:],
                           _rope_slab(kr[:, :half], kr[:, half:])], axis=1).astype(BF16)
    wq = w_q_b.reshape(Q_LORA, MLA_HEADS, QK_HEAD)
    wqb = jnp.concatenate([wq[..., :QK_NOPE],
                           _rope_slab(wq[..., QK_NOPE:QK_NOPE + half], wq[..., QK_NOPE + half:])],
                          axis=-1).reshape(Q_LORA, MLA_HEADS * MLA_QK_PAD).astype(BF16)
    wkv = w_kv_b.reshape(KV_LORA, MLA_HEADS, QK_NOPE + V_HEAD)
    wkvb = jnp.concatenate([wkv[..., :QK_NOPE].reshape(KV_LORA, -1),
                            wkv[..., QK_NOPE:].reshape(KV_LORA, -1)], axis=1).astype(BF16)

    def gain(g):
        return jnp.concatenate([g[:QK_NOPE], _rope_slab(g[QK_NOPE:QK_NOPE + half],
                                                        g[QK_NOPE + half:])])[None, :]

    gq = gain(q_norm) * (QK_HEAD ** -0.5)
    gk = gain(k_norm)
    cos, sin = _rope_tables(seq_len, QK_ROPE)
    return win, wqb, wkvb, gq, gk, _rope_slab(cos, cos), _rope_slab(-sin, sin)


def _trunk(x, w):
    b, s, d = x.shape

    def ffn(x, layer, which):
        y = _ffn(x.reshape(b * s, d), w["ffn_norm"][layer, which][None, :], w["wg"][layer, which],
                 w["wu"][layer, which], w["wd"][layer, which])
        return y.reshape(b, s, d)

    x = ffn(x, 0, 0)
    win, wqb, wkvb, gq, gk, cos, sin = w["mla"]
    q, k, v, gb, cu = _mla_proj(x, w["mix_norm"][0][None, :], win, w["qag"], w["kvag"], wqb, wkvb,
                                gq, gk, cos, sin)
    attn = _mla_attn(q, k, v)
    x = _mla_out(attn, gb, cu, w["conv_w"], w["ab_wo"], x)
    x = ffn(x, 0, 1)
    x = ffn(x, 1, 0)
    cwin, cgq, cgk, ccos, csin, sink = w["gqa"]
    q, k, v = _gqa_proj(x, w["mix_norm"][1][None, :], cwin, cgq, cgk, ccos, csin)
    o = _win_attn(q, k, v, sink)
    x = _proj_residual(o, w["c_wo"], x)
    x = ffn(x, 1, 1)
    return x


def kernel(x_prompt, x_sample, ffn_norm, ffn_w_gate, ffn_w_up, ffn_w_down, mix_norm, ab_w_in, ab_q_a_norm, ab_w_q_b, ab_kv_a_norm, ab_w_kv_b, ab_q_norm, ab_k_norm, ab_conv_w, ab_w_out, c_w_in, c_q_norm, c_k_norm, c_sink, c_w_out):
    s = x_prompt.shape[1]
    cos, sin = _rope_tables(s, C_HEAD_DIM)
    w = {
        "ffn_norm": ffn_norm,
        "wg": ffn_w_gate.astype(BF16),
        "wu": ffn_w_up.astype(BF16),
        "wd": ffn_w_down.astype(BF16),
        "mix_norm": mix_norm,
        "mla": _prep_mla(ab_w_in[0], ab_w_q_b[0], ab_w_kv_b[0], ab_q_norm[0], ab_k_norm[0], s),
        "qag": ab_q_a_norm[0][None, :],
        "kvag": ab_kv_a_norm[0][None, :],
        "conv_w": ab_conv_w[0],
        "ab_wo": ab_w_out[0].astype(BF16),
        "gqa": (c_w_in[0].astype(BF16),
                c_q_norm[0][None, :] * (C_HEAD_DIM ** -0.5),
                c_k_norm[0][None, :],
                jnp.concatenate([cos, cos], axis=-1),
                jnp.concatenate([-sin, sin], axis=-1),
                jnp.broadcast_to(c_sink[0].reshape(C_KV_HEADS, C_GROUP, 1),
                                 (C_KV_HEADS, C_GROUP, LANES))),
        "c_wo": c_w_out[0].astype(BF16),
    }
    return _trunk(x_prompt, w), _trunk(x_sample, w)
```

```python
import functools

import jax
import jax.numpy as jnp
from jax import lax
from jax.experimental import pallas as pl
from jax.experimental.pallas import tpu as pltpu

D_MODEL = 2048
D_FF = 5632
MLA_HEADS = 8
Q_LORA = 512
KV_LORA = 512
QK_NOPE = 128
QK_ROPE = 64
QK_HEAD = QK_NOPE + QK_ROPE
V_HEAD = 128
MLA_WIDTH = MLA_HEADS * V_HEAD
CONV_WIDTH = D_MODEL - MLA_WIDTH
C_HEADS = 16
C_KV_HEADS = 4
C_GROUP = C_HEADS // C_KV_HEADS
C_HEAD_DIM = 128
WINDOW = 128
ROPE_THETA = 10000.0
EPS = 1e-6
NEG = -1e30
LOG2E = 1.4426950408889634

LANES = 128
MLA_QK_PAD = 2 * LANES
AB_IN_PAD = Q_LORA + KV_LORA + 3 * CONV_WIDTH + LANES
VMEM_LIMIT = 56 * 1024 * 1024

F32 = jnp.float32
BF16 = jnp.bfloat16


def _params(*sem):
    return pltpu.CompilerParams(dimension_semantics=sem, vmem_limit_bytes=VMEM_LIMIT)


def _resident(shape):
    nd = len(shape)
    return pl.BlockSpec(shape, lambda *_: (0,) * nd, pipeline_mode=pl.Buffered(1))


def _rms(x, gain):
    ms = jnp.mean(x * x, axis=-1, keepdims=True)
    return x * lax.rsqrt(ms + EPS) * gain


def _ffn_body(x_ref, g_ref, wg_ref, wu_ref, wd_ref, o_ref, h_ref):
    j = pl.program_id(1)

    @pl.when(j == 0)
    def _():
        x = x_ref[...]
        h_ref[...] = _rms(x, g_ref[...]).astype(BF16)
        o_ref[...] = x

    h = h_ref[...]
    gate = jnp.dot(h, wg_ref[...], preferred_element_type=F32)
    up = jnp.dot(h, wu_ref[...], preferred_element_type=F32)
    act = (gate * jax.nn.sigmoid(gate) * up * 0.5).astype(BF16)
    o_ref[...] += jnp.dot(act, wd_ref[...], preferred_element_type=F32)


FFN_TM = 1024
FFN_TF = 256


def _ffn(x, gain, wg, wu, wd, *, tm=FFN_TM):
    t, d = x.shape
    nf, _, tf = wg.shape
    return pl.pallas_call(
        _ffn_body,
        grid=(t // tm, nf),
        in_specs=[
            pl.BlockSpec((tm, d), lambda i, j: (i, 0)),
            pl.BlockSpec((1, d), lambda i, j: (0, 0)),
            pl.BlockSpec((None, d, tf), lambda i, j: (j, 0, 0)),
            pl.BlockSpec((None, d, tf), lambda i, j: (j, 0, 0)),
            pl.BlockSpec((tf, d), lambda i, j: (j, 0)),
        ],
        out_specs=pl.BlockSpec((tm, d), lambda i, j: (i, 0)),
        out_shape=jax.ShapeDtypeStruct((t, d), F32),
        scratch_shapes=[pltpu.VMEM((tm, d), BF16)],
        compiler_params=_params("parallel", "arbitrary"),
        name="ffn",
    )(x, gain, wg, wu, wd)


def _mla_proj_body(x_ref, g_ref, win_ref, qag_ref, kvag_ref, wqb_ref, wkvb_ref, gq_ref, gk_ref,
                   cos_ref, sin_ref, q_ref, k_ref, v_ref, gb_ref, cu_ref):
    h = _rms(x_ref[...], g_ref[...]).astype(BF16)
    z = jnp.dot(h, win_ref[...], preferred_element_type=F32)
    c0 = Q_LORA + KV_LORA
    gb_ref[...] = z[:, c0:c0 + CONV_WIDTH].astype(BF16)
    cu_ref[...] = (z[:, c0 + CONV_WIDTH:c0 + 2 * CONV_WIDTH]
                   * z[:, c0 + 2 * CONV_WIDTH:c0 + 3 * CONV_WIDTH]).astype(BF16)
    k_rope = z[:, c0 + 3 * CONV_WIDTH:]

    qa = _rms(z[:, :Q_LORA], qag_ref[...]).astype(BF16)
    kva = _rms(z[:, Q_LORA:c0], kvag_ref[...]).astype(BF16)
    q = jnp.dot(qa, wqb_ref[...], preferred_element_type=F32)
    kv = jnp.dot(kva, wkvb_ref[...], preferred_element_type=F32)
    cos = cos_ref[...]
    sin = sin_ref[...]
    gq = gq_ref[...]
    gk = gk_ref[...]

    def rope(t):
        return t * cos + pltpu.roll(t, LANES // 2, 1) * sin

    kr_rot = rope(k_rope * gk[:, LANES:])
    kr_ss = jnp.sum(k_rope * k_rope, axis=-1, keepdims=True)
    for hd in range(MLA_HEADS):
        lo = hd * MLA_QK_PAD
        qn = q[:, lo:lo + LANES]
        qr = q[:, lo + LANES:lo + MLA_QK_PAD]
        ss = jnp.sum(qn * qn, axis=-1, keepdims=True) + jnp.sum(qr * qr, axis=-1, keepdims=True)
        r = lax.rsqrt(ss * (1.0 / QK_HEAD) + EPS)
        q_ref[hd, :, :LANES] = (qn * r * gq[:, :LANES]).astype(BF16)
        q_ref[hd, :, LANES:] = rope(qr * r * gq[:, LANES:]).astype(BF16)
        kn = kv[:, hd * QK_NOPE:(hd + 1) * QK_NOPE]
        ssk = jnp.sum(kn * kn, axis=-1, keepdims=True) + kr_ss
        rk = lax.rsqrt(ssk * (1.0 / QK_HEAD) + EPS)
        k_ref[hd, :, :LANES] = (kn * rk * gk[:, :LANES]).astype(BF16)
        k_ref[hd, :, LANES:] = (kr_rot * rk).astype(BF16)
        v0 = MLA_HEADS * QK_NOPE + hd * V_HEAD
        v_ref[hd] = kv[:, v0:v0 + V_HEAD].astype(BF16)


def _mla_proj(x, gain, win, qag, kvag, wqb, wkvb, gq, gk, cos, sin, *, tm=256):
    b, s, d = x.shape
    tok = lambda w: pl.BlockSpec((None, tm, w), lambda bi, i: (bi, i, 0))
    bf = lambda w: jax.ShapeDtypeStruct((b, s, w), BF16)
    head = lambda w: pl.BlockSpec((None, MLA_HEADS, tm, w), lambda bi, i: (bi, 0, i, 0))
    hbf = lambda w: jax.ShapeDtypeStruct((b, MLA_HEADS, s, w), BF16)
    return pl.pallas_call(
        _mla_proj_body,
        grid=(b, s // tm),
        in_specs=[
            tok(d),
            _resident((1, d)),
            _resident(win.shape),
            _resident((1, Q_LORA)),
            _resident((1, KV_LORA)),
            _resident(wqb.shape),
            _resident(wkvb.shape),
            _resident((1, MLA_QK_PAD)),
            _resident((1, MLA_QK_PAD)),
            pl.BlockSpec((tm, LANES), lambda bi, i: (i, 0)),
            pl.BlockSpec((tm, LANES), lambda bi, i: (i, 0)),
        ],
        out_specs=[head(MLA_QK_PAD), head(MLA_QK_PAD), head(V_HEAD), tok(CONV_WIDTH), tok(CONV_WIDTH)],
        out_shape=[hbf(MLA_QK_PAD), hbf(MLA_QK_PAD), hbf(V_HEAD), bf(CONV_WIDTH), bf(CONV_WIDTH)],
        compiler_params=_params("parallel", "parallel"),
        name="mla_proj",
    )(x, gain, win, qag, kvag, wqb, wkvb, gq, gk, cos, sin)


MLA_TQ = 256


def _mla_attn_body(q_ref, k_ref, v_ref, o_ref, sa_ref, sb_ref, v1_ref):
    n_tiles = q_ref.shape[0] // MLA_TQ

    def scores(t):
        q = q_ref[pl.ds(pl.multiple_of(t * MLA_TQ, MLA_TQ), MLA_TQ), :]
        return lax.dot_general(q, k_ref[...], (((1,), (1,)), ((), ())), preferred_element_type=F32)

    def finish(s_ref, t):
        s = s_ref[...]
        p = jnp.exp2(s - jnp.max(s, axis=-1, keepdims=True)).astype(BF16)
        o = jnp.dot(p, v1_ref[...], preferred_element_type=F32)
        rows = pl.ds(pl.multiple_of(t * MLA_TQ, MLA_TQ), MLA_TQ)
        o_ref[rows, :] = (o[:, :V_HEAD] / o[:, V_HEAD:]).astype(BF16)

    v1_ref[:, :V_HEAD] = v_ref[...]
    v1_ref[:, V_HEAD:] = jnp.ones((v1_ref.shape[0], V_HEAD), BF16)
    sa_ref[...] = scores(0)

    def pair(j, carry):
        sb_ref[...] = scores(2 * j + 1)
        finish(sa_ref, 2 * j)
        sa_ref[...] = scores(2 * j + 2)
        finish(sb_ref, 2 * j + 1)
        return carry

    lax.fori_loop(0, n_tiles // 2 - 1, pair, 0)
    sb_ref[...] = scores(n_tiles - 1)
    finish(sa_ref, n_tiles - 2)
    finish(sb_ref, n_tiles - 1)


def _mla_attn(q, k, v):
    b, h, s, _ = q.shape
    head = lambda w: pl.BlockSpec((None, None, s, w), lambda bi, hi: (bi, hi, 0, 0))
    return pl.pallas_call(
        _mla_attn_body,
        grid=(b, h),
        in_specs=[head(MLA_QK_PAD), head(MLA_QK_PAD), head(V_HEAD)],
        out_specs=pl.BlockSpec((None, s, V_HEAD), lambda bi, hi: (bi, 0, hi)),
        out_shape=jax.ShapeDtypeStruct((b, s, h * V_HEAD), BF16),
        scratch_shapes=[pltpu.VMEM((MLA_TQ, s), F32), pltpu.VMEM((MLA_TQ, s), F32),
                        pltpu.VMEM((s, 2 * V_HEAD), BF16)],
        compiler_params=_params("parallel", "parallel"),
        name="mla_attn",
    )(q, k, v)


CONV_HALO = 16


def _mla_out_body(attn_ref, gb_ref, cu_ref, prev_ref, next_ref, cw_ref, wo_ref, x_ref, o_ref):
    i = pl.program_id(1)
    tm = cu_ref.shape[0]
    cu = cu_ref[...].astype(F32)
    prev_row = prev_ref[...].astype(F32)[CONV_HALO - 1:CONV_HALO, :] * (i > 0).astype(F32)
    next_row = next_ref[...].astype(F32)[0:1, :] * (i < pl.num_programs(1) - 1).astype(F32)
    row = lax.broadcasted_iota(jnp.int32, (tm, 1), 0)
    before = jnp.where(row == 0, prev_row, pltpu.roll(cu, 1, 0))
    after = jnp.where(row == tm - 1, next_row, pltpu.roll(cu, tm - 1, 0))
    cw = cw_ref[...]
    conv = before * cw[0:1, :] + cu * cw[1:2, :] + after * cw[2:3, :]
    conv_out = (gb_ref[...].astype(F32) * conv).astype(BF16)
    y = jnp.dot(attn_ref[...], wo_ref[:MLA_WIDTH, :], preferred_element_type=F32)
    y += jnp.dot(conv_out, wo_ref[MLA_WIDTH:, :], preferred_element_type=F32)
    o_ref[...] = x_ref[...] + y


def _mla_out(attn, gb, cu, conv_w, wo, x, *, tm=512):
    b, s, d = x.shape
    hb = tm // CONV_HALO
    last = s // CONV_HALO - 1
    tok = lambda w: pl.BlockSpec((None, tm, w), lambda bi, i: (bi, i, 0))
    return pl.pallas_call(
        _mla_out_body,
        grid=(b, s // tm),
        in_specs=[
            tok(MLA_WIDTH), tok(CONV_WIDTH), tok(CONV_WIDTH),
            pl.BlockSpec((None, CONV_HALO, CONV_WIDTH),
                         lambda bi, i: (bi, jnp.maximum(i * hb - 1, 0), 0)),
            pl.BlockSpec((None, CONV_HALO, CONV_WIDTH),
                         lambda bi, i: (bi, jnp.minimum((i + 1) * hb, last), 0)),
            _resident(conv_w.shape),
            _resident(wo.shape),
            tok(d),
        ],
        out_specs=tok(d),
        out_shape=jax.ShapeDtypeStruct((b, s, d), F32),
        compiler_params=_params("parallel", "parallel"),
        name="mla_out",
    )(attn, gb, cu, cu, cu, conv_w, wo, x)


def _gqa_proj_body(x_ref, g_ref, win_ref, gq_ref, gk_ref, cos_ref, sin_ref, q_ref, k_ref, v_ref):
    h = _rms(x_ref[...], g_ref[...]).astype(BF16)
    z = jnp.dot(h, win_ref[...], preferred_element_type=F32)
    cos = cos_ref[...]
    sin = sin_ref[...]
    gq = gq_ref[...]
    gk = gk_ref[...]

    def norm_rope(t, gain):
        t = _rms(t, gain)
        return (t * cos + pltpu.roll(t, C_HEAD_DIM // 2, 1) * sin).astype(BF16)

    for hd in range(C_HEADS):
        lo = hd * C_HEAD_DIM
        q_ref[hd] = norm_rope(z[:, lo:lo + C_HEAD_DIM], gq)
    k0 = C_HEADS * C_HEAD_DIM
    v0 = k0 + C_KV_HEADS * C_HEAD_DIM
    for hd in range(C_KV_HEADS):
        lo = hd * C_HEAD_DIM
        k_ref[hd] = norm_rope(z[:, k0 + lo:k0 + lo + C_HEAD_DIM], gk)
        v_ref[hd] = z[:, v0 + lo:v0 + lo + C_HEAD_DIM].astype(BF16)


def _gqa_proj(x, gain, win, gq, gk, cos, sin, *, tm=256):
    b, s, d = x.shape
    tok = lambda w: pl.BlockSpec((None, tm, w), lambda bi, i: (bi, i, 0))
    head = lambda n: pl.BlockSpec((None, n, tm, C_HEAD_DIM), lambda bi, i: (bi, 0, i, 0))
    hbf = lambda n: jax.ShapeDtypeStruct((b, n, s, C_HEAD_DIM), BF16)
    return pl.pallas_call(
        _gqa_proj_body,
        grid=(b, s // tm),
        in_specs=[
            tok(d),
            _resident((1, d)),
            _resident(win.shape),
            _resident((1, C_HEAD_DIM)),
            _resident((1, C_HEAD_DIM)),
            pl.BlockSpec((tm, C_HEAD_DIM), lambda bi, i: (i, 0)),
            pl.BlockSpec((tm, C_HEAD_DIM), lambda bi, i: (i, 0)),
        ],
        out_specs=[head(C_HEADS), head(C_KV_HEADS), head(C_KV_HEADS)],
        out_shape=[hbf(C_HEADS), hbf(C_KV_HEADS), hbf(C_KV_HEADS)],
        compiler_params=_params("parallel", "parallel"),
        name="gqa_proj",
    )(x, gain, win, gq, gk, cos, sin)


WIN_TQ = 256
WIN_KW = WIN_TQ + 2 * WINDOW


def _win_attn_body(q_ref, k_ref, v_ref, sink_ref, o_ref, sa_ref, sb_ref, v1_ref):
    s_len = k_ref.shape[0]
    n_tiles = s_len // WIN_TQ
    sink = sink_ref[...][:, 0:1].reshape(C_GROUP, 1, 1)

    def key_start(t):
        return pl.multiple_of(jnp.clip(t * WIN_TQ - WINDOW, 0, s_len - WIN_KW), WINDOW)

    def scores(t):
        rows = pl.ds(pl.multiple_of(t * WIN_TQ, WIN_TQ), WIN_TQ)
        q = q_ref[:, rows, :].reshape(C_GROUP * WIN_TQ, C_HEAD_DIM)
        kk = k_ref[pl.ds(key_start(t), WIN_KW), :]
        return lax.dot_general(q, kk, (((1,), (1,)), ((), ())), preferred_element_type=F32)

    def finish(s_ref, t):
        start = key_start(t)
        qpos = t * WIN_TQ + lax.broadcasted_iota(jnp.int32, (WIN_TQ, 1), 0)
        kpos = start + lax.broadcasted_iota(jnp.int32, (1, WIN_KW), 1)
        lim = jnp.where(jnp.abs(qpos - kpos) <= WINDOW, jnp.inf, NEG).astype(F32)
        s = jnp.minimum(s_ref[...].reshape(C_GROUP, WIN_TQ, WIN_KW), lim[None])
        m = jnp.maximum(jnp.max(s, axis=-1, keepdims=True), sink)
        p = jnp.exp2(s - m).astype(BF16).reshape(C_GROUP * WIN_TQ, WIN_KW)
        o = jnp.dot(p, v1_ref[pl.ds(start, WIN_KW), :], preferred_element_type=F32)
        l = o[:, C_HEAD_DIM:] + jnp.exp2(sink - m).reshape(C_GROUP * WIN_TQ, 1)
        out = (o[:, :C_HEAD_DIM] / l).astype(BF16)
        rows = pl.ds(pl.multiple_of(t * WIN_TQ, WIN_TQ), WIN_TQ)
        for g in range(C_GROUP):
            o_ref[rows, g * C_HEAD_DIM:(g + 1) * C_HEAD_DIM] = out[g * WIN_TQ:(g + 1) * WIN_TQ]

    v1_ref[:, :C_HEAD_DIM] = v_ref[...]
    v1_ref[:, C_HEAD_DIM:] = jnp.ones((s_len, C_HEAD_DIM), BF16)
    sa_ref[...] = scores(0)

    def pair(j, carry):
        sb_ref[...] = scores(2 * j + 1)
        finish(sa_ref, 2 * j)
        sa_ref[...] = scores(2 * j + 2)
        finish(sb_ref, 2 * j + 1)
        return carry

    lax.fori_loop(0, n_tiles // 2 - 1, pair, 0)
    sb_ref[...] = scores(n_tiles - 1)
    finish(sa_ref, n_tiles - 2)
    finish(sb_ref, n_tiles - 1)


def _win_attn(q, k, v, sink):
    b, _, s, _ = q.shape
    kv = pl.BlockSpec((None, None, s, C_HEAD_DIM), lambda bi, h: (bi, h, 0, 0))
    return pl.pallas_call(
        _win_attn_body,
        grid=(b, C_KV_HEADS),
        in_specs=[
            pl.BlockSpec((None, C_GROUP, s, C_HEAD_DIM), lambda bi, h: (bi, h, 0, 0)),
            kv, kv,
            pl.BlockSpec((None, C_GROUP, LANES), lambda bi, h: (h, 0, 0)),
        ],
        out_specs=pl.BlockSpec((None, s, C_GROUP * C_HEAD_DIM), lambda bi, h: (bi, 0, h)),
        out_shape=jax.ShapeDtypeStruct((b, s, C_HEADS * C_HEAD_DIM), BF16),
        scratch_shapes=[pltpu.VMEM((C_GROUP * WIN_TQ, WIN_KW), F32),
                        pltpu.VMEM((C_GROUP * WIN_TQ, WIN_KW), F32),
                        pltpu.VMEM((s, 2 * C_HEAD_DIM), BF16)],
        compiler_params=_params("parallel", "parallel"),
        name="win_attn",
    )(q, k, v, sink)


def _proj_residual_body(a_ref, w_ref, x_ref, o_ref):
    o_ref[...] = x_ref[...] + jnp.dot(a_ref[...], w_ref[...], preferred_element_type=F32)


def _proj_residual(a, w, x, *, tm=512):
    b, s, d = x.shape
    tok = lambda width: pl.BlockSpec((None, tm, width), lambda bi, i: (bi, i, 0))
    return pl.pallas_call(
        _proj_residual_body,
        grid=(b, s // tm),
        in_specs=[tok(a.shape[-1]), _resident(w.shape), tok(d)],
        out_specs=tok(d),
        out_shape=jax.ShapeDtypeStruct((b, s, d), F32),
        compiler_params=_params("parallel", "parallel"),
        name="proj_residual",
    )(a, w, x)


def _col_tiles(w, tf):
    *lead, d, f = w.shape
    w = w.reshape(*lead, d, f // tf, tf)
    return jnp.swapaxes(w, -3, -2)


def _rope_tables(seq_len, dim):
    inv = 1.0 / (ROPE_THETA ** (jnp.arange(0, dim, 2, dtype=F32) / dim))
    ang = jnp.arange(seq_len, dtype=F32)[:, None] * inv[None, :]
    return jnp.cos(ang), jnp.sin(ang)


def _rope_slab(x1, x2):
    z = jnp.zeros_like(x1)
    return jnp.concatenate([x1, z, x2, z], axis=-1)


def _prep_mla(w_in, w_q_b, w_kv_b, q_norm, k_norm, seq_len):
    half = QK_ROPE // 2
    c0 = Q_LORA + KV_LORA
    kr = w_in[:, c0:c0 + QK_ROPE]
    win = jnp.concatenate([w_in[:, :c0], w_in[:, c0 + QK_ROPE:],
                           _rope_slab(kr[:, :half], kr[:, half:])], axis=1).astype(BF16)
    wq = w_q_b.reshape(Q_LORA, MLA_HEADS, QK_HEAD)
    wqb = jnp.concatenate([wq[..., :QK_NOPE],
                           _rope_slab(wq[..., QK_NOPE:QK_NOPE + half], wq[..., QK_NOPE + half:])],
                          axis=-1).reshape(Q_LORA, MLA_HEADS * MLA_QK_PAD).astype(BF16)
    wkv = w_kv_b.reshape(KV_LORA, MLA_HEADS, QK_NOPE + V_HEAD)
    wkvb = jnp.concatenate([wkv[..., :QK_NOPE].reshape(KV_LORA, -1),
                            wkv[..., QK_NOPE:].reshape(KV_LORA, -1)], axis=1).astype(BF16)

    def gain(g):
        return jnp.concatenate([g[:QK_NOPE], _rope_slab(g[QK_NOPE:QK_NOPE + half],
                                                        g[QK_NOPE + half:])])[None, :]

    gq = gain(q_norm) * (QK_HEAD ** -0.5 * LOG2E)
    gk = gain(k_norm)
    cos, sin = _rope_tables(seq_len, QK_ROPE)
    return win, wqb, wkvb, gq, gk, _rope_slab(cos, cos), _rope_slab(-sin, sin)


def _trunk(x, w):
    b, s, d = x.shape

    def ffn(x, layer, which):
        y = _ffn(x.reshape(b * s, d), w["ffn_norm"][layer, which][None, :], w["wg"][layer, which],
                 w["wu"][layer, which], w["wd"][layer, which])
        return y.reshape(b, s, d)

    x = ffn(x, 0, 0)
    win, wqb, wkvb, gq, gk, cos, sin = w["mla"]
    q, k, v, gb, cu = _mla_proj(x, w["mix_norm"][0][None, :], win, w["qag"], w["kvag"], wqb, wkvb,
                                gq, gk, cos, sin)
    attn = _mla_attn(q, k, v)
    x = _mla_out(attn, gb, cu, w["conv_w"], w["ab_wo"], x)
    x = ffn(x, 0, 1)
    x = ffn(x, 1, 0)
    cwin, cgq, cgk, ccos, csin, sink = w["gqa"]
    q, k, v = _gqa_proj(x, w["mix_norm"][1][None, :], cwin, cgq, cgk, ccos, csin)
    o = _win_attn(q, k, v, sink)
    x = _proj_residual(o, w["c_wo"], x)
    x = ffn(x, 1, 1)
    return x


def kernel(x_prompt, x_sample, ffn_norm, ffn_w_gate, ffn_w_up, ffn_w_down, mix_norm, ab_w_in, ab_q_a_norm, ab_w_q_b, ab_kv_a_norm, ab_w_kv_b, ab_q_norm, ab_k_norm, ab_conv_w, ab_w_out, c_w_in, c_q_norm, c_k_norm, c_sink, c_w_out):
    s = x_prompt.shape[1]
    cos, sin = _rope_tables(s, C_HEAD_DIM)
    w = {
        "ffn_norm": ffn_norm,
        "wg": _col_tiles(ffn_w_gate.astype(BF16), FFN_TF),
        "wu": _col_tiles(ffn_w_up.astype(BF16), FFN_TF),
        "wd": ffn_w_down.astype(BF16),
        "mix_norm": mix_norm,
        "mla": _prep_mla(ab_w_in[0], ab_w_q_b[0], ab_w_kv_b[0], ab_q_norm[0], ab_k_norm[0], s),
        "qag": ab_q_a_norm[0][None, :],
        "kvag": ab_kv_a_norm[0][None, :],
        "conv_w": ab_conv_w[0],
        "ab_wo": ab_w_out[0].astype(BF16),
        "gqa": (c_w_in[0].astype(BF16),
                c_q_norm[0][None, :] * (C_HEAD_DIM ** -0.5 * LOG2E),
                c_k_norm[0][None, :],
                jnp.concatenate([cos, cos], axis=-1),
                jnp.concatenate([-sin, sin], axis=-1),
                jnp.broadcast_to((c_sink[0] * LOG2E).reshape(C_KV_HEADS, C_GROUP, 1),
                                 (C_KV_HEADS, C_GROUP, LANES))),
        "c_wo": c_w_out[0].astype(BF16),
    }
    return _trunk(x_prompt, w), _trunk(x_sample, w)
```

```python
import functools

import jax
import jax.numpy as jnp
from jax import lax
from jax.experimental import pallas as pl
from jax.experimental.pallas import tpu as pltpu

D_MODEL = 2048
D_FF = 5632
MLA_HEADS = 8
Q_LORA = 512
KV_LORA = 512
QK_NOPE = 128
QK_ROPE = 64
QK_HEAD = QK_NOPE + QK_ROPE
V_HEAD = 128
MLA_WIDTH = MLA_HEADS * V_HEAD
CONV_WIDTH = D_MODEL - MLA_WIDTH
C_HEADS = 16
C_KV_HEADS = 4
C_GROUP = C_HEADS // C_KV_HEADS
C_HEAD_DIM = 128
WINDOW = 128
ROPE_THETA = 10000.0
EPS = 1e-6
NEG = -1e30
LOG2E = 1.4426950408889634

LANES = 128
MLA_QK_PAD = 2 * LANES
AB_IN_PAD = Q_LORA + KV_LORA + 3 * CONV_WIDTH + LANES
VMEM_LIMIT = 56 * 1024 * 1024

F32 = jnp.float32
BF16 = jnp.bfloat16


def _params(*sem, vmem=VMEM_LIMIT):
    return pltpu.CompilerParams(dimension_semantics=sem, vmem_limit_bytes=vmem)


def _resident(shape):
    nd = len(shape)
    return pl.BlockSpec(shape, lambda *_: (0,) * nd, pipeline_mode=pl.Buffered(1))


def _rms(x, gain):
    ms = jnp.mean(x * x, axis=-1, keepdims=True)
    return x * lax.rsqrt(ms + EPS) * gain


def _ffn_body(x_ref, g_ref, wg_ref, wu_ref, wd_ref, o_ref, h_ref):
    j = pl.program_id(1)

    @pl.when(j == 0)
    def _():
        x = x_ref[...]
        h_ref[...] = _rms(x, g_ref[...]).astype(BF16)
        o_ref[...] = x

    h = h_ref[...]
    half = wg_ref.shape[1] // 2

    def act(cols):
        gate = jnp.dot(h, wg_ref[:, cols], preferred_element_type=F32)
        up = jnp.dot(h, wu_ref[:, cols], preferred_element_type=F32)
        return (gate * jax.nn.sigmoid(gate) * up * 0.5).astype(BF16)

    a = jnp.concatenate([act(slice(0, half)), act(slice(half, 2 * half))], axis=1)
    o_ref[...] += jnp.dot(a, wd_ref[...], preferred_element_type=F32)


FFN_TM = 1024
FFN_TF = 512
FFN_VMEM_LIMIT = 62 * 1024 * 1024


def _ffn(x, gain, wg, wu, wd, layer, which, *, tm=FFN_TM, tf=FFN_TF):
    t, d = x.shape
    f = wg.shape[-1]
    return pl.pallas_call(
        _ffn_body,
        grid=(t // tm, f // tf),
        in_specs=[
            pl.BlockSpec((tm, d), lambda i, j: (i, 0)),
            pl.BlockSpec((None, None, 1, d), lambda i, j: (layer, which, 0, 0)),
            pl.BlockSpec((None, None, d, tf), lambda i, j: (layer, which, 0, j)),
            pl.BlockSpec((None, None, d, tf), lambda i, j: (layer, which, 0, j)),
            pl.BlockSpec((None, None, tf, d), lambda i, j: (layer, which, j, 0)),
        ],
        out_specs=pl.BlockSpec((tm, d), lambda i, j: (i, 0)),
        out_shape=jax.ShapeDtypeStruct((t, d), F32),
        scratch_shapes=[pltpu.VMEM((tm, d), BF16)],
        compiler_params=_params("parallel", "arbitrary", vmem=FFN_VMEM_LIMIT),
        name="ffn",
    )(x, gain, wg, wu, wd)


def _mla_proj_body(x_ref, g_ref, win_ref, qag_ref, kvag_ref, wqb_ref, wkvb_ref, gq_ref, gk_ref,
                   cos_ref, sin_ref, q_ref, k_ref, v_ref, gb_ref, cu_ref):
    h = _rms(x_ref[...], g_ref[...]).astype(BF16)
    c0 = Q_LORA + KV_LORA
    c1 = c0 + LANES
    z = jnp.dot(h, win_ref[:, :c1], preferred_element_type=F32)
    k_rope = z[:, c0:c1]

    qa = _rms(z[:, :Q_LORA], qag_ref[...]).astype(BF16)
    kva = _rms(z[:, Q_LORA:c0], kvag_ref[...]).astype(BF16)
    q = jnp.dot(qa, wqb_ref[...], preferred_element_type=F32)
    kv = jnp.dot(kva, wkvb_ref[...], preferred_element_type=F32)
    cos = cos_ref[...]
    sin = sin_ref[...]
    gq = gq_ref[...]
    gk = gk_ref[...]

    def rope(t):
        return t * cos + pltpu.roll(t, LANES // 2, 1) * sin

    kr_rot = rope(k_rope * gk[:, LANES:])
    kr_ss = jnp.sum(k_rope * k_rope, axis=-1, keepdims=True)
    for hd in range(MLA_HEADS):
        lo = hd * MLA_QK_PAD
        qn = q[:, lo:lo + LANES]
        qr = q[:, lo + LANES:lo + MLA_QK_PAD]
        ss = jnp.sum(qn * qn, axis=-1, keepdims=True) + jnp.sum(qr * qr, axis=-1, keepdims=True)
        r = lax.rsqrt(ss * (1.0 / QK_HEAD) + EPS)
        q_ref[hd, :, :LANES] = (qn * r * gq[:, :LANES]).astype(BF16)
        q_ref[hd, :, LANES:] = rope(qr * r * gq[:, LANES:]).astype(BF16)
        kn = kv[:, hd * QK_NOPE:(hd + 1) * QK_NOPE]
        ssk = jnp.sum(kn * kn, axis=-1, keepdims=True) + kr_ss
        rk = lax.rsqrt(ssk * (1.0 / QK_HEAD) + EPS)
        k_ref[hd, :, :LANES] = (kn * rk * gk[:, :LANES]).astype(BF16)
        k_ref[hd, :, LANES:] = (kr_rot * rk).astype(BF16)
        v0 = MLA_HEADS * QK_NOPE + hd * V_HEAD
        v_ref[hd] = kv[:, v0:v0 + V_HEAD].astype(BF16)

    zc = jnp.dot(h, win_ref[:, c1:], preferred_element_type=F32)
    gb_ref[...] = zc[:, :CONV_WIDTH].astype(BF16)
    cu_ref[...] = (zc[:, CONV_WIDTH:2 * CONV_WIDTH] * zc[:, 2 * CONV_WIDTH:]).astype(BF16)


def _mla_proj(x, gain, win, qag, kvag, wqb, wkvb, gq, gk, cos, sin, *, tm=256):
    b, s, d = x.shape
    tok = lambda w: pl.BlockSpec((None, tm, w), lambda bi, i: (bi, i, 0))
    bf = lambda w: jax.ShapeDtypeStruct((b, s, w), BF16)
    head = lambda w: pl.BlockSpec((None, MLA_HEADS, tm, w), lambda bi, i: (bi, 0, i, 0))
    hbf = lambda w: jax.ShapeDtypeStruct((b, MLA_HEADS, s, w), BF16)
    return pl.pallas_call(
        _mla_proj_body,
        grid=(b, s // tm),
        in_specs=[
            tok(d),
            _resident((1, d)),
            _resident(win.shape),
            _resident((1, Q_LORA)),
            _resident((1, KV_LORA)),
            _resident(wqb.shape),
            _resident(wkvb.shape),
            _resident((1, MLA_QK_PAD)),
            _resident((1, MLA_QK_PAD)),
            pl.BlockSpec((tm, LANES), lambda bi, i: (i, 0)),
            pl.BlockSpec((tm, LANES), lambda bi, i: (i, 0)),
        ],
        out_specs=[head(MLA_QK_PAD), head(MLA_QK_PAD), head(V_HEAD), tok(CONV_WIDTH), tok(CONV_WIDTH)],
        out_shape=[hbf(MLA_QK_PAD), hbf(MLA_QK_PAD), hbf(V_HEAD), bf(CONV_WIDTH), bf(CONV_WIDTH)],
        compiler_params=_params("parallel", "parallel"),
        name="mla_proj",
    )(x, gain, win, qag, kvag, wqb, wkvb, gq, gk, cos, sin)


MLA_TQ = 256


def _mla_attn_body(q_ref, k_ref, v_ref, o_ref, sa_ref, sb_ref, v1_ref):
    n_tiles = q_ref.shape[0] // MLA_TQ

    def scores(t):
        q = q_ref[pl.ds(pl.multiple_of(t * MLA_TQ, MLA_TQ), MLA_TQ), :]
        return lax.dot_general(q, k_ref[...], (((1,), (1,)), ((), ())), preferred_element_type=F32)

    def finish(s_ref, t):
        s = s_ref[...]
        p = jnp.exp2(s - jnp.max(s, axis=-1, keepdims=True)).astype(BF16)
        o = jnp.dot(p, v1_ref[...], preferred_element_type=F32)
        rows = pl.ds(pl.multiple_of(t * MLA_TQ, MLA_TQ), MLA_TQ)
        o_ref[rows, :] = (o[:, :V_HEAD] / o[:, V_HEAD:]).astype(BF16)

    v1_ref[:, :V_HEAD] = v_ref[...]
    v1_ref[:, V_HEAD:] = jnp.ones((v1_ref.shape[0], V_HEAD), BF16)
    sa_ref[...] = scores(0)

    def pair(j, carry):
        sb_ref[...] = scores(2 * j + 1)
        finish(sa_ref, 2 * j)
        sa_ref[...] = scores(2 * j + 2)
        finish(sb_ref, 2 * j + 1)
        return carry

    lax.fori_loop(0, n_tiles // 2 - 1, pair, 0)
    sb_ref[...] = scores(n_tiles - 1)
    finish(sa_ref, n_tiles - 2)
    finish(sb_ref, n_tiles - 1)


def _mla_attn(q, k, v):
    b, h, s, _ = q.shape
    head = lambda w: pl.BlockSpec((None, None, s, w), lambda bi, hi: (bi, hi, 0, 0))
    return pl.pallas_call(
        _mla_attn_body,
        grid=(b, h),
        in_specs=[head(MLA_QK_PAD), head(MLA_QK_PAD), head(V_HEAD)],
        out_specs=pl.BlockSpec((None, s, V_HEAD), lambda bi, hi: (bi, 0, hi)),
        out_shape=jax.ShapeDtypeStruct((b, s, h * V_HEAD), BF16),
        scratch_shapes=[pltpu.VMEM((MLA_TQ, s), F32), pltpu.VMEM((MLA_TQ, s), F32),
                        pltpu.VMEM((s, 2 * V_HEAD), BF16)],
        compiler_params=_params("parallel", "parallel"),
        name="mla_attn",
    )(q, k, v)


CONV_HALO = 16


def _mla_out_body(attn_ref, gb_ref, cu_ref, prev_ref, next_ref, cw_ref, wo_ref, x_ref, o_ref):
    i = pl.program_id(1)
    tm = cu_ref.shape[0]
    cu = cu_ref[...].astype(F32)
    prev_row = prev_ref[...].astype(F32)[CONV_HALO - 1:CONV_HALO, :] * (i > 0).astype(F32)
    next_row = next_ref[...].astype(F32)[0:1, :] * (i < pl.num_programs(1) - 1).astype(F32)
    row = lax.broadcasted_iota(jnp.int32, (tm, 1), 0)
    before = jnp.where(row == 0, prev_row, pltpu.roll(cu, 1, 0))
    after = jnp.where(row == tm - 1, next_row, pltpu.roll(cu, tm - 1, 0))
    cw = cw_ref[...]
    conv = before * cw[0:1, :] + cu * cw[1:2, :] + after * cw[2:3, :]
    conv_out = (gb_ref[...].astype(F32) * conv).astype(BF16)
    y = jnp.dot(attn_ref[...], wo_ref[:MLA_WIDTH, :], preferred_element_type=F32)
    y += jnp.dot(conv_out, wo_ref[MLA_WIDTH:, :], preferred_element_type=F32)
    o_ref[...] = x_ref[...] + y


def _mla_out(attn, gb, cu, conv_w, wo, x, *, tm=512):
    b, s, d = x.shape
    hb = tm // CONV_HALO
    last = s // CONV_HALO - 1
    tok = lambda w: pl.BlockSpec((None, tm, w), lambda bi, i: (bi, i, 0))
    return pl.pallas_call(
        _mla_out_body,
        grid=(b, s // tm),
        in_specs=[
            tok(MLA_WIDTH), tok(CONV_WIDTH), tok(CONV_WIDTH),
            pl.BlockSpec((None, CONV_HALO, CONV_WIDTH),
                         lambda bi, i: (bi, jnp.maximum(i * hb - 1, 0), 0)),
            pl.BlockSpec((None, CONV_HALO, CONV_WIDTH),
                         lambda bi, i: (bi, jnp.minimum((i + 1) * hb, last), 0)),
            _resident(conv_w.shape),
            _resident(wo.shape),
            tok(d),
        ],
        out_specs=tok(d),
        out_shape=jax.ShapeDtypeStruct((b, s, d), F32),
        compiler_params=_params("parallel", "parallel"),
        name="mla_out",
    )(attn, gb, cu, cu, cu, conv_w, wo, x)


def _gqa_proj_body(x_ref, g_ref, win_ref, gq_ref, gk_ref, cos_ref, sin_ref, q_ref, k_ref, v_ref):
    h = _rms(x_ref[...], g_ref[...]).astype(BF16)
    cos = cos_ref[...]
    sin = sin_ref[...]
    gq = gq_ref[...]
    gk = gk_ref[...]

    def norm_rope(t, gain):
        t = _rms(t, gain)
        return (t * cos + pltpu.roll(t, C_HEAD_DIM // 2, 1) * sin).astype(BF16)

    def heads(group):
        lo = group * C_GROUP * C_HEAD_DIM
        z = jnp.dot(h, win_ref[:, lo:lo + C_GROUP * C_HEAD_DIM], preferred_element_type=F32)
        return [z[:, i * C_HEAD_DIM:(i + 1) * C_HEAD_DIM] for i in range(C_GROUP)]

    for group in range(C_KV_HEADS):
        for i, t in enumerate(heads(group)):
            q_ref[group * C_GROUP + i] = norm_rope(t, gq)
    for hd, t in enumerate(heads(C_KV_HEADS)):
        k_ref[hd] = norm_rope(t, gk)
    for hd, t in enumerate(heads(C_KV_HEADS + 1)):
        v_ref[hd] = t.astype(BF16)


def _gqa_proj(x, gain, win, gq, gk, cos, sin, *, tm=256):
    b, s, d = x.shape
    tok = lambda w: pl.BlockSpec((None, tm, w), lambda bi, i: (bi, i, 0))
    head = lambda n: pl.BlockSpec((None, n, tm, C_HEAD_DIM), lambda bi, i: (bi, 0, i, 0))
    hbf = lambda n: jax.ShapeDtypeStruct((b, n, s, C_HEAD_DIM), BF16)
    return pl.pallas_call(
        _gqa_proj_body,
        grid=(b, s // tm),
        in_specs=[
            tok(d),
            _resident((1, d)),
            _resident(win.shape),
            _resident((1, C_HEAD_DIM)),
            _resident((1, C_HEAD_DIM)),
            pl.BlockSpec((tm, C_HEAD_DIM), lambda bi, i: (i, 0)),
            pl.BlockSpec((tm, C_HEAD_DIM), lambda bi, i: (i, 0)),
        ],
        out_specs=[head(C_HEADS), head(C_KV_HEADS), head(C_KV_HEADS)],
        out_shape=[hbf(C_HEADS), hbf(C_KV_HEADS), hbf(C_KV_HEADS)],
        compiler_params=_params("parallel", "parallel"),
        name="gqa_proj",
    )(x, gain, win, gq, gk, cos, sin)


WIN_TQ = 256
WIN_KW = WIN_TQ + 2 * WINDOW


def _win_attn_body(q_ref, k_ref, v_ref, sink_ref, o_ref, sa_ref, sb_ref, v1_ref):
    s_len = k_ref.shape[0]
    n_tiles = s_len // WIN_TQ
    sink = sink_ref[...][:, 0:1].reshape(C_GROUP, 1, 1)

    def key_start(t):
        return pl.multiple_of(jnp.clip(t * WIN_TQ - WINDOW, 0, s_len - WIN_KW), WINDOW)

    def scores(t):
        rows = pl.ds(pl.multiple_of(t * WIN_TQ, WIN_TQ), WIN_TQ)
        q = q_ref[:, rows, :].reshape(C_GROUP * WIN_TQ, C_HEAD_DIM)
        kk = k_ref[pl.ds(key_start(t), WIN_KW), :]
        return lax.dot_general(q, kk, (((1,), (1,)), ((), ())), preferred_element_type=F32)

    def finish(s_ref, t):
        start = key_start(t)
        qpos = t * WIN_TQ + lax.broadcasted_iota(jnp.int32, (WIN_TQ, 1), 0)
        kpos = start + lax.broadcasted_iota(jnp.int32, (1, WIN_KW), 1)
        lim = jnp.where(jnp.abs(qpos - kpos) <= WINDOW, jnp.inf, NEG).astype(F32)
        s = jnp.minimum(s_ref[...].reshape(C_GROUP, WIN_TQ, WIN_KW), lim[None])
        m = jnp.maximum(jnp.max(s, axis=-1, keepdims=True), sink)
        p = jnp.exp2(s - m).astype(BF16).reshape(C_GROUP * WIN_TQ, WIN_KW)
        o = jnp.dot(p, v1_ref[pl.ds(start, WIN_KW), :], preferred_element_type=F32)
        l = o[:, C_HEAD_DIM:] + jnp.exp2(sink - m).reshape(C_GROUP * WIN_TQ, 1)
        out = (o[:, :C_HEAD_DIM] / l).astype(BF16)
        rows = pl.ds(pl.multiple_of(t * WIN_TQ, WIN_TQ), WIN_TQ)
        for g in range(C_GROUP):
            o_ref[rows, g * C_HEAD_DIM:(g + 1) * C_HEAD_DIM] = out[g * WIN_TQ:(g + 1) * WIN_TQ]

    v1_ref[:, :C_HEAD_DIM] = v_ref[...]
    v1_ref[:, C_HEAD_DIM:] = jnp.ones((s_len, C_HEAD_DIM), BF16)
    sa_ref[...] = scores(0)

    def pair(j, carry):
        sb_ref[...] = scores(2 * j + 1)
        finish(sa_ref, 2 * j)
        sa_ref[...] = scores(2 * j + 2)
        finish(sb_ref, 2 * j + 1)
        return carry

    lax.fori_loop(0, n_tiles // 2 - 1, pair, 0)
    sb_ref[...] = scores(n_tiles - 1)
    finish(sa_ref, n_tiles - 2)
    finish(sb_ref, n_tiles - 1)


def _win_attn(q, k, v, sink):
    b, _, s, _ = q.shape
    kv = pl.BlockSpec((None, None, s, C_HEAD_DIM), lambda bi, h: (bi, h, 0, 0))
    return pl.pallas_call(
        _win_attn_body,
        grid=(b, C_KV_HEADS),
        in_specs=[
            pl.BlockSpec((None, C_GROUP, s, C_HEAD_DIM), lambda bi, h: (bi, h, 0, 0)),
            kv, kv,
            pl.BlockSpec((None, C_GROUP, LANES), lambda bi, h: (h, 0, 0)),
        ],
        out_specs=pl.BlockSpec((None, s, C_GROUP * C_HEAD_DIM), lambda bi, h: (bi, 0, h)),
        out_shape=jax.ShapeDtypeStruct((b, s, C_HEADS * C_HEAD_DIM), BF16),
        scratch_shapes=[pltpu.VMEM((C_GROUP * WIN_TQ, WIN_KW), F32),
                        pltpu.VMEM((C_GROUP * WIN_TQ, WIN_KW), F32),
                        pltpu.VMEM((s, 2 * C_HEAD_DIM), BF16)],
        compiler_params=_params("parallel", "parallel"),
        name="win_attn",
    )(q, k, v, sink)


def _proj_residual_body(a_ref, w_ref, x_ref, o_ref):
    o_ref[...] = x_ref[...] + jnp.dot(a_ref[...], w_ref[...], preferred_element_type=F32)


def _proj_residual(a, w, x, *, tm=512):
    b, s, d = x.shape
    tok = lambda width: pl.BlockSpec((None, tm, width), lambda bi, i: (bi, i, 0))
    return pl.pallas_call(
        _proj_residual_body,
        grid=(b, s // tm),
        in_specs=[tok(a.shape[-1]), _resident(w.shape), tok(d)],
        out_specs=tok(d),
        out_shape=jax.ShapeDtypeStruct((b, s, d), F32),
        compiler_params=_params("parallel", "parallel"),
        name="proj_residual",
    )(a, w, x)


def _rope_tables(seq_len, dim):
    inv = 1.0 / (ROPE_THETA ** (jnp.arange(0, dim, 2, dtype=F32) / dim))
    ang = jnp.arange(seq_len, dtype=F32)[:, None] * inv[None, :]
    return jnp.cos(ang), jnp.sin(ang)


def _rope_slab(x1, x2):
    z = jnp.zeros_like(x1)
    return jnp.concatenate([x1, z, x2, z], axis=-1)


def _prep_mla(w_in, w_q_b, w_kv_b, q_norm, k_norm, seq_len):
    half = QK_ROPE // 2
    c0 = Q_LORA + KV_LORA
    kr = w_in[:, c0:c0 + QK_ROPE]
    win = jnp.concatenate([w_in[:, :c0], _rope_slab(kr[:, :half], kr[:, half:]),
                           w_in[:, c0 + QK_ROPE:]], axis=1).astype(BF16)
    wq = w_q_b.reshape(Q_LORA, MLA_HEADS, QK_HEAD)
    wqb = jnp.concatenate([wq[..., :QK_NOPE],
                           _rope_slab(wq[..., QK_NOPE:QK_NOPE + half], wq[..., QK_NOPE + half:])],
                          axis=-1).reshape(Q_LORA, MLA_HEADS * MLA_QK_PAD).astype(BF16)
    wkv = w_kv_b.reshape(KV_LORA, MLA_HEADS, QK_NOPE + V_HEAD)
    wkvb = jnp.concatenate([wkv[..., :QK_NOPE].reshape(KV_LORA, -1),
                            wkv[..., QK_NOPE:].reshape(KV_LORA, -1)], axis=1).astype(BF16)

    def gain(g):
        return jnp.concatenate([g[:QK_NOPE], _rope_slab(g[QK_NOPE:QK_NOPE + half],
                                                        g[QK_NOPE + half:])])[None, :]

    gq = gain(q_norm) * (QK_HEAD ** -0.5 * LOG2E)
    gk = gain(k_norm)
    cos, sin = _rope_tables(seq_len, QK_ROPE)
    return win, wqb, wkvb, gq, gk, _rope_slab(cos, cos), _rope_slab(-sin, sin)


def _trunk(x, w):
    b, s, d = x.shape

    def ffn(x, layer, which):
        y = _ffn(x.reshape(b * s, d), w["ffn_norm"], w["wg"], w["wu"], w["wd"], layer, which)
        return y.reshape(b, s, d)

    x = ffn(x, 0, 0)
    win, wqb, wkvb, gq, gk, cos, sin = w["mla"]
    q, k, v, gb, cu = _mla_proj(x, w["mix_norm"][0][None, :], win, w["qag"], w["kvag"], wqb, wkvb,
                                gq, gk, cos, sin)
    attn = _mla_attn(q, k, v)
    x = _mla_out(attn, gb, cu, w["conv_w"], w["ab_wo"], x)
    x = ffn(x, 0, 1)
    x = ffn(x, 1, 0)
    cwin, cgq, cgk, ccos, csin, sink = w["gqa"]
    q, k, v = _gqa_proj(x, w["mix_norm"][1][None, :], cwin, cgq, cgk, ccos, csin)
    o = _win_attn(q, k, v, sink)
    x = _proj_residual(o, w["c_wo"], x)
    x = ffn(x, 1, 1)
    return x


def kernel(x_prompt, x_sample, ffn_norm, ffn_w_gate, ffn_w_up, ffn_w_down, mix_norm, ab_w_in, ab_q_a_norm, ab_w_q_b, ab_kv_a_norm, ab_w_kv_b, ab_q_norm, ab_k_norm, ab_conv_w, ab_w_out, c_w_in, c_q_norm, c_k_norm, c_sink, c_w_out):
    s = x_prompt.shape[1]
    cos, sin = _rope_tables(s, C_HEAD_DIM)
    w = {
        "ffn_norm": ffn_norm[:, :, None, :],
        "wg": ffn_w_gate.astype(BF16),
        "wu": ffn_w_up.astype(BF16),
        "wd": ffn_w_down.astype(BF16),
        "mix_norm": mix_norm,
        "mla": _prep_mla(ab_w_in[0], ab_w_q_b[0], ab_w_kv_b[0], ab_q_norm[0], ab_k_norm[0], s),
        "qag": ab_q_a_norm[0][None, :],
        "kvag": ab_kv_a_norm[0][None, :],
        "conv_w": ab_conv_w[0],
        "ab_wo": ab_w_out[0].astype(BF16),
        "gqa": (c_w_in[0].astype(BF16),
                c_q_norm[0][None, :] * (C_HEAD_DIM ** -0.5 * LOG2E),
                c_k_norm[0][None, :],
                jnp.concatenate([cos, cos], axis=-1),
                jnp.concatenate([-sin, sin], axis=-1),
                jnp.broadcast_to((c_sink[0] * LOG2E).reshape(C_KV_HEADS, C_GROUP, 1),
                                 (C_KV_HEADS, C_GROUP, LANES))),
        "c_wo": c_w_out[0].astype(BF16),
    }
    return _trunk(x_prompt, w), _trunk(x_sample, w)
```

```python
import functools

import jax
import jax.numpy as jnp
from jax import lax
from jax.experimental import pallas as pl
from jax.experimental.pallas import tpu as pltpu

D_MODEL = 2048
D_FF = 5632
MLA_HEADS = 8
Q_LORA = 512
KV_LORA = 512
QK_NOPE = 128
QK_ROPE = 64
QK_HEAD = QK_NOPE + QK_ROPE
V_HEAD = 128
MLA_WIDTH = MLA_HEADS * V_HEAD
CONV_WIDTH = D_MODEL - MLA_WIDTH
C_HEADS = 16
C_KV_HEADS = 4
C_GROUP = C_HEADS // C_KV_HEADS
C_HEAD_DIM = 128
WINDOW = 128
ROPE_THETA = 10000.0
EPS = 1e-6
NEG = -1e30
LOG2E = 1.4426950408889634

LANES = 128
MLA_QK_PAD = 2 * LANES
AB_IN_PAD = Q_LORA + KV_LORA + 3 * CONV_WIDTH + LANES
VMEM_LIMIT = 56 * 1024 * 1024

F32 = jnp.float32
BF16 = jnp.bfloat16


def _params(*sem, vmem=VMEM_LIMIT):
    return pltpu.CompilerParams(dimension_semantics=sem, vmem_limit_bytes=vmem)


def _resident(shape):
    nd = len(shape)
    return pl.BlockSpec(shape, lambda *_: (0,) * nd, pipeline_mode=pl.Buffered(1))


def _rms(x, gain):
    ms = jnp.mean(x * x, axis=-1, keepdims=True)
    return x * lax.rsqrt(ms + EPS) * gain


def _ffn_body(x_hbm, g_ref, wg_ref, wu_ref, wd_ref, o_ref, xbuf, hbuf, sem, *, chunk):
    i = pl.program_id(0)
    j = pl.program_id(1)
    n = pl.num_programs(0)
    tm = xbuf.shape[1]
    slot = i % 2
    nslot = 1 - slot
    gain = g_ref[...]

    def x_copy(tile, s):
        return pltpu.make_async_copy(x_hbm.at[pl.ds(tile * tm, tm), :], xbuf.at[s], sem.at[s])

    @pl.when((i == 0) & (j == 0))
    def _():
        x_copy(0, 0).start()
        x_copy(1, 1).start()
        x_copy(0, 0).wait()
        hbuf[0] = _rms(xbuf[0], gain).astype(BF16)
        x_copy(1, 1).wait()

    @pl.when((i >= 1) & (j == 0) & (i + 1 < n))
    def _():
        x_copy(i + 1, nslot).wait()

    @pl.when((j == 1) & (i + 2 < n))
    def _():
        x_copy(i + 2, slot).start()

    def step(first):
        h = hbuf[slot]
        half = wg_ref.shape[1] // 2

        def act(cols):
            gate = jnp.dot(h, wg_ref[:, cols], preferred_element_type=F32)
            up = jnp.dot(h, wu_ref[:, cols], preferred_element_type=F32)
            return (gate * jax.nn.sigmoid(gate) * up * 0.5).astype(BF16)

        a = jnp.concatenate([act(slice(0, half)), act(slice(half, 2 * half))], axis=1)
        y = jnp.dot(a, wd_ref[...], preferred_element_type=F32)
        if first:
            o_ref[...] = xbuf[slot] + y
        else:
            o_ref[...] += y
        rows = pl.ds(pl.multiple_of(jnp.minimum(j * chunk, tm - chunk), 16), chunk)
        hbuf[nslot, rows, :] = _rms(xbuf[nslot, rows, :], gain).astype(BF16)

    pl.when(j == 0)(functools.partial(step, True))
    pl.when(j > 0)(functools.partial(step, False))


FFN_TM = 1024
FFN_TF = 512
FFN_VMEM_LIMIT = 62 * 1024 * 1024
BF16_SUBLANES = 16


def _ffn(x, gain, wg, wu, wd, layer, which, *, tf=FFN_TF):
    t, d = x.shape
    f = wg.shape[-1]
    tm = min(FFN_TM, t // 2)
    nf = f // tf
    chunk = -(-pl.cdiv(tm, nf) // BF16_SUBLANES) * BF16_SUBLANES
    assert t % tm == 0 and f % tf == 0 and chunk <= tm and tm % BF16_SUBLANES == 0
    return pl.pallas_call(
        functools.partial(_ffn_body, chunk=chunk),
        grid=(t // tm, nf),
        in_specs=[
            pl.BlockSpec(memory_space=pl.ANY),
            pl.BlockSpec((None, None, 1, d), lambda i, j: (layer, which, 0, 0)),
            pl.BlockSpec((None, None, d, tf), lambda i, j: (layer, which, 0, j)),
            pl.BlockSpec((None, None, d, tf), lambda i, j: (layer, which, 0, j)),
            pl.BlockSpec((None, None, tf, d), lambda i, j: (layer, which, j, 0)),
        ],
        out_specs=pl.BlockSpec((tm, d), lambda i, j: (i, 0)),
        out_shape=jax.ShapeDtypeStruct((t, d), F32),
        scratch_shapes=[pltpu.VMEM((2, tm, d), F32), pltpu.VMEM((2, tm, d), BF16),
                        pltpu.SemaphoreType.DMA((2,))],
        compiler_params=_params("arbitrary", "arbitrary", vmem=FFN_VMEM_LIMIT),
        name="ffn",
    )(x, gain, wg, wu, wd)


def _mla_proj_body(x_ref, g_ref, win_ref, qag_ref, kvag_ref, wqb_ref, wkvb_ref, gq_ref, gk_ref,
                   cos_ref, sin_ref, q_ref, k_ref, v_ref, gb_ref, cu_ref):
    h = _rms(x_ref[...], g_ref[...]).astype(BF16)
    c0 = Q_LORA + KV_LORA
    c1 = c0 + LANES
    z = jnp.dot(h, win_ref[:, :c1], preferred_element_type=F32)
    k_rope = z[:, c0:c1]

    qa = _rms(z[:, :Q_LORA], qag_ref[...]).astype(BF16)
    kva = _rms(z[:, Q_LORA:c0], kvag_ref[...]).astype(BF16)
    q = jnp.dot(qa, wqb_ref[...], preferred_element_type=F32)
    kv = jnp.dot(kva, wkvb_ref[...], preferred_element_type=F32)
    cos = cos_ref[...]
    sin = sin_ref[...]
    gq = gq_ref[...]
    gk = gk_ref[...]

    def rope(t):
        return t * cos + pltpu.roll(t, LANES // 2, 1) * sin

    kr_rot = rope(k_rope * gk[:, LANES:])
    kr_ss = jnp.sum(k_rope * k_rope, axis=-1, keepdims=True)
    for hd in range(MLA_HEADS):
        lo = hd * MLA_QK_PAD
        qn = q[:, lo:lo + LANES]
        qr = q[:, lo + LANES:lo + MLA_QK_PAD]
        ss = jnp.sum(qn * qn, axis=-1, keepdims=True) + jnp.sum(qr * qr, axis=-1, keepdims=True)
        r = lax.rsqrt(ss * (1.0 / QK_HEAD) + EPS)
        q_ref[hd, :, :LANES] = (qn * r * gq[:, :LANES]).astype(BF16)
        q_ref[hd, :, LANES:] = rope(qr * r * gq[:, LANES:]).astype(BF16)
        kn = kv[:, hd * QK_NOPE:(hd + 1) * QK_NOPE]
        ssk = jnp.sum(kn * kn, axis=-1, keepdims=True) + kr_ss
        rk = lax.rsqrt(ssk * (1.0 / QK_HEAD) + EPS)
        k_ref[hd, :, :LANES] = (kn * rk * gk[:, :LANES]).astype(BF16)
        k_ref[hd, :, LANES:] = (kr_rot * rk).astype(BF16)
        v0 = MLA_HEADS * QK_NOPE + hd * V_HEAD
        v_ref[hd] = kv[:, v0:v0 + V_HEAD].astype(BF16)

    zc = jnp.dot(h, win_ref[:, c1:], preferred_element_type=F32)
    gb_ref[...] = zc[:, :CONV_WIDTH].astype(BF16)
    cu_ref[...] = (zc[:, CONV_WIDTH:2 * CONV_WIDTH] * zc[:, 2 * CONV_WIDTH:]).astype(BF16)


def _mla_proj(x, gain, win, qag, kvag, wqb, wkvb, gq, gk, cos, sin, *, tm=256):
    b, s, d = x.shape
    tok = lambda w: pl.BlockSpec((None, tm, w), lambda bi, i: (bi, i, 0))
    bf = lambda w: jax.ShapeDtypeStruct((b, s, w), BF16)
    head = lambda w: pl.BlockSpec((None, MLA_HEADS, tm, w), lambda bi, i: (bi, 0, i, 0))
    hbf = lambda w: jax.ShapeDtypeStruct((b, MLA_HEADS, s, w), BF16)
    return pl.pallas_call(
        _mla_proj_body,
        grid=(b, s // tm),
        in_specs=[
            tok(d),
            _resident((1, d)),
            _resident(win.shape),
            _resident((1, Q_LORA)),
            _resident((1, KV_LORA)),
            _resident(wqb.shape),
            _resident(wkvb.shape),
            _resident((1, MLA_QK_PAD)),
            _resident((1, MLA_QK_PAD)),
            pl.BlockSpec((tm, LANES), lambda bi, i: (i, 0)),
            pl.BlockSpec((tm, LANES), lambda bi, i: (i, 0)),
        ],
        out_specs=[head(MLA_QK_PAD), head(MLA_QK_PAD), head(V_HEAD), tok(CONV_WIDTH), tok(CONV_WIDTH)],
        out_shape=[hbf(MLA_QK_PAD), hbf(MLA_QK_PAD), hbf(V_HEAD), bf(CONV_WIDTH), bf(CONV_WIDTH)],
        compiler_params=_params("parallel", "parallel"),
        name="mla_proj",
    )(x, gain, win, qag, kvag, wqb, wkvb, gq, gk, cos, sin)


MLA_TQ = 256


def _mla_attn_body(q_ref, k_ref, v_ref, o_ref, sa_ref, sb_ref, v1_ref):
    n_tiles = q_ref.shape[0] // MLA_TQ

    def scores(t):
        q = q_ref[pl.ds(pl.multiple_of(t * MLA_TQ, MLA_TQ), MLA_TQ), :]
        return lax.dot_general(q, k_ref[...], (((1,), (1,)), ((), ())), preferred_element_type=F32)

    def finish(s_ref, t):
        s = s_ref[...]
        p = jnp.exp2(s - jnp.max(s, axis=-1, keepdims=True)).astype(BF16)
        o = jnp.dot(p, v1_ref[...], preferred_element_type=F32)
        rows = pl.ds(pl.multiple_of(t * MLA_TQ, MLA_TQ), MLA_TQ)
        o_ref[rows, :] = (o[:, :V_HEAD] / o[:, V_HEAD:]).astype(BF16)

    v1_ref[:, :V_HEAD] = v_ref[...]
    v1_ref[:, V_HEAD:] = jnp.ones((v1_ref.shape[0], V_HEAD), BF16)
    sa_ref[...] = scores(0)

    def pair(j, carry):
        sb_ref[...] = scores(2 * j + 1)
        finish(sa_ref, 2 * j)
        sa_ref[...] = scores(2 * j + 2)
        finish(sb_ref, 2 * j + 1)
        return carry

    lax.fori_loop(0, n_tiles // 2 - 1, pair, 0)
    sb_ref[...] = scores(n_tiles - 1)
    finish(sa_ref, n_tiles - 2)
    finish(sb_ref, n_tiles - 1)


def _mla_attn(q, k, v):
    b, h, s, _ = q.shape
    head = lambda w: pl.BlockSpec((None, None, s, w), lambda bi, hi: (bi, hi, 0, 0))
    return pl.pallas_call(
        _mla_attn_body,
        grid=(b, h),
        in_specs=[head(MLA_QK_PAD), head(MLA_QK_PAD), head(V_HEAD)],
        out_specs=pl.BlockSpec((None, s, V_HEAD), lambda bi, hi: (bi, 0, hi)),
        out_shape=jax.ShapeDtypeStruct((b, s, h * V_HEAD), BF16),
        scratch_shapes=[pltpu.VMEM((MLA_TQ, s), F32), pltpu.VMEM((MLA_TQ, s), F32),
                        pltpu.VMEM((s, 2 * V_HEAD), BF16)],
        compiler_params=_params("parallel", "parallel"),
        name="mla_attn",
    )(q, k, v)


CONV_HALO = 16


def _mla_out_body(attn_ref, gb_ref, cu_ref, prev_ref, next_ref, cw_ref, wo_ref, x_ref, o_ref):
    i = pl.program_id(1)
    tm = cu_ref.shape[0]
    cu = cu_ref[...].astype(F32)
    prev_row = prev_ref[...].astype(F32)[CONV_HALO - 1:CONV_HALO, :] * (i > 0).astype(F32)
    next_row = next_ref[...].astype(F32)[0:1, :] * (i < pl.num_programs(1) - 1).astype(F32)
    row = lax.broadcasted_iota(jnp.int32, (tm, 1), 0)
    before = jnp.where(row == 0, prev_row, pltpu.roll(cu, 1, 0))
    after = jnp.where(row == tm - 1, next_row, pltpu.roll(cu, tm - 1, 0))
    cw = cw_ref[...]
    conv = before * cw[0:1, :] + cu * cw[1:2, :] + after * cw[2:3, :]
    conv_out = (gb_ref[...].astype(F32) * conv).astype(BF16)
    y = jnp.dot(attn_ref[...], wo_ref[:MLA_WIDTH, :], preferred_element_type=F32)
    y += jnp.dot(conv_out, wo_ref[MLA_WIDTH:, :], preferred_element_type=F32)
    o_ref[...] = x_ref[...] + y


def _mla_out(attn, gb, cu, conv_w, wo, x, *, tm=512):
    b, s, d = x.shape
    hb = tm // CONV_HALO
    last = s // CONV_HALO - 1
    tok = lambda w: pl.BlockSpec((None, tm, w), lambda bi, i: (bi, i, 0))
    return pl.pallas_call(
        _mla_out_body,
        grid=(b, s // tm),
        in_specs=[
            tok(MLA_WIDTH), tok(CONV_WIDTH), tok(CONV_WIDTH),
            pl.BlockSpec((None, CONV_HALO, CONV_WIDTH),
                         lambda bi, i: (bi, jnp.maximum(i * hb - 1, 0), 0)),
            pl.BlockSpec((None, CONV_HALO, CONV_WIDTH),
                         lambda bi, i: (bi, jnp.minimum((i + 1) * hb, last), 0)),
            _resident(conv_w.shape),
            _resident(wo.shape),
            tok(d),
        ],
        out_specs=tok(d),
        out_shape=jax.ShapeDtypeStruct((b, s, d), F32),
        compiler_params=_params("parallel", "parallel"),
        name="mla_out",
    )(attn, gb, cu, cu, cu, conv_w, wo, x)


def _gqa_proj_body(x_ref, g_ref, win_ref, gq_ref, gk_ref, cos_ref, sin_ref, q_ref, k_ref, v_ref):
    h = _rms(x_ref[...], g_ref[...]).astype(BF16)
    cos = cos_ref[...]
    sin = sin_ref[...]
    gq = gq_ref[...]
    gk = gk_ref[...]

    def norm_rope(t, gain):
        t = _rms(t, gain)
        return (t * cos + pltpu.roll(t, C_HEAD_DIM // 2, 1) * sin).astype(BF16)

    def heads(group):
        lo = group * C_GROUP * C_HEAD_DIM
        z = jnp.dot(h, win_ref[:, lo:lo + C_GROUP * C_HEAD_DIM], preferred_element_type=F32)
        return [z[:, i * C_HEAD_DIM:(i + 1) * C_HEAD_DIM] for i in range(C_GROUP)]

    for group in range(C_KV_HEADS):
        for i, t in enumerate(heads(group)):
            q_ref[group * C_GROUP + i] = norm_rope(t, gq)
    for hd, t in enumerate(heads(C_KV_HEADS)):
        k_ref[hd] = norm_rope(t, gk)
    for hd, t in enumerate(heads(C_KV_HEADS + 1)):
        v_ref[hd] = t.astype(BF16)


def _gqa_proj(x, gain, win, gq, gk, cos, sin, *, tm=256):
    b, s, d = x.shape
    tok = lambda w: pl.BlockSpec((None, tm, w), lambda bi, i: (bi, i, 0))
    head = lambda n: pl.BlockSpec((None, n, tm, C_HEAD_DIM), lambda bi, i: (bi, 0, i, 0))
    hbf = lambda n: jax.ShapeDtypeStruct((b, n, s, C_HEAD_DIM), BF16)
    return pl.pallas_call(
        _gqa_proj_body,
        grid=(b, s // tm),
        in_specs=[
            tok(d),
            _resident((1, d)),
            _resident(win.shape),
            _resident((1, C_HEAD_DIM)),
            _resident((1, C_HEAD_DIM)),
            pl.BlockSpec((tm, C_HEAD_DIM), lambda bi, i: (i, 0)),
            pl.BlockSpec((tm, C_HEAD_DIM), lambda bi, i: (i, 0)),
        ],
        out_specs=[head(C_HEADS), head(C_KV_HEADS), head(C_KV_HEADS)],
        out_shape=[hbf(C_HEADS), hbf(C_KV_HEADS), hbf(C_KV_HEADS)],
        compiler_params=_params("parallel", "parallel"),
        name="gqa_proj",
    )(x, gain, win, gq, gk, cos, sin)


WIN_TQ = 256
WIN_KW = WIN_TQ + 2 * WINDOW


def _win_attn_body(q_ref, k_ref, v_ref, sink_ref, o_ref, sa_ref, sb_ref, v1_ref, lim_ref):
    s_len = k_ref.shape[0]
    n_tiles = s_len // WIN_TQ
    sink = sink_ref[...][:, 0:1].reshape(C_GROUP, 1, 1)

    def key_start(t):
        return pl.multiple_of(jnp.clip(t * WIN_TQ - WINDOW, 0, s_len - WIN_KW), WINDOW)

    def scores(t):
        rows = pl.ds(pl.multiple_of(t * WIN_TQ, WIN_TQ), WIN_TQ)
        q = q_ref[:, rows, :].reshape(C_GROUP * WIN_TQ, C_HEAD_DIM)
        kk = k_ref[pl.ds(key_start(t), WIN_KW), :]
        return lax.dot_general(q, kk, (((1,), (1,)), ((), ())), preferred_element_type=F32)

    def finish(s_ref, t):
        start = key_start(t)
        lim = lim_ref[(t * WIN_TQ - start) // WINDOW]
        s = jnp.minimum(s_ref[...].reshape(C_GROUP, WIN_TQ, WIN_KW), lim[None])
        m = jnp.maximum(jnp.max(s, axis=-1, keepdims=True), sink)
        p = jnp.exp2(s - m).astype(BF16).reshape(C_GROUP * WIN_TQ, WIN_KW)
        o = jnp.dot(p, v1_ref[pl.ds(start, WIN_KW), :], preferred_element_type=F32)
        l = o[:, C_HEAD_DIM:] + jnp.exp2(sink - m).reshape(C_GROUP * WIN_TQ, 1)
        out = (o[:, :C_HEAD_DIM] / l).astype(BF16)
        rows = pl.ds(pl.multiple_of(t * WIN_TQ, WIN_TQ), WIN_TQ)
        for g in range(C_GROUP):
            o_ref[rows, g * C_HEAD_DIM:(g + 1) * C_HEAD_DIM] = out[g * WIN_TQ:(g + 1) * WIN_TQ]

    rel = (lax.broadcasted_iota(jnp.int32, (WIN_TQ, 1), 0)
           - lax.broadcasted_iota(jnp.int32, (1, WIN_KW), 1))
    for k in range(3):
        lim_ref[k] = jnp.where(jnp.abs(rel + k * WINDOW) <= WINDOW, jnp.inf, NEG).astype(F32)
    v1_ref[:, :C_HEAD_DIM] = v_ref[...]
    v1_ref[:, C_HEAD_DIM:] = jnp.ones((s_len, C_HEAD_DIM), BF16)
    sa_ref[...] = scores(0)

    def pair(j, carry):
        sb_ref[...] = scores(2 * j + 1)
        finish(sa_ref, 2 * j)
        sa_ref[...] = scores(2 * j + 2)
        finish(sb_ref, 2 * j + 1)
        return carry

    lax.fori_loop(0, n_tiles // 2 - 1, pair, 0)
    sb_ref[...] = scores(n_tiles - 1)
    finish(sa_ref, n_tiles - 2)
    finish(sb_ref, n_tiles - 1)


def _win_attn(q, k, v, sink):
    b, _, s, _ = q.shape
    kv = pl.BlockSpec((None, None, s, C_HEAD_DIM), lambda bi, h: (bi, h, 0, 0))
    return pl.pallas_call(
        _win_attn_body,
        grid=(b, C_KV_HEADS),
        in_specs=[
            pl.BlockSpec((None, C_GROUP, s, C_HEAD_DIM), lambda bi, h: (bi, h, 0, 0)),
            kv, kv,
            pl.BlockSpec((None, C_GROUP, LANES), lambda bi, h: (h, 0, 0)),
        ],
        out_specs=pl.BlockSpec((None, s, C_GROUP * C_HEAD_DIM), lambda bi, h: (bi, 0, h)),
        out_shape=jax.ShapeDtypeStruct((b, s, C_HEADS * C_HEAD_DIM), BF16),
        scratch_shapes=[pltpu.VMEM((C_GROUP * WIN_TQ, WIN_KW), F32),
                        pltpu.VMEM((C_GROUP * WIN_TQ, WIN_KW), F32),
                        pltpu.VMEM((s, 2 * C_HEAD_DIM), BF16),
                        pltpu.VMEM((3, WIN_TQ, WIN_KW), F32)],
        compiler_params=_params("parallel", "parallel"),
        name="win_attn",
    )(q, k, v, sink)


def _proj_residual_body(a_ref, w_ref, x_ref, o_ref):
    o_ref[...] = x_ref[...] + jnp.dot(a_ref[...], w_ref[...], preferred_element_type=F32)


def _proj_residual(a, w, x, *, tm=512):
    b, s, d = x.shape
    tok = lambda width: pl.BlockSpec((None, tm, width), lambda bi, i: (bi, i, 0))
    return pl.pallas_call(
        _proj_residual_body,
        grid=(b, s // tm),
        in_specs=[tok(a.shape[-1]), _resident(w.shape), tok(d)],
        out_specs=tok(d),
        out_shape=jax.ShapeDtypeStruct((b, s, d), F32),
        compiler_params=_params("parallel", "parallel"),
        name="proj_residual",
    )(a, w, x)


def _rope_tables(seq_len, dim):
    inv = 1.0 / (ROPE_THETA ** (jnp.arange(0, dim, 2, dtype=F32) / dim))
    ang = jnp.arange(seq_len, dtype=F32)[:, None] * inv[None, :]
    return jnp.cos(ang), jnp.sin(ang)


def _rope_slab(x1, x2):
    z = jnp.zeros_like(x1)
    return jnp.concatenate([x1, z, x2, z], axis=-1)


def _prep_mla(w_in, w_q_b, w_kv_b, q_norm, k_norm, seq_len):
    half = QK_ROPE // 2
    c0 = Q_LORA + KV_LORA
    kr = w_in[:, c0:c0 + QK_ROPE]
    win = jnp.concatenate([w_in[:, :c0], _rope_slab(kr[:, :half], kr[:, half:]),
                           w_in[:, c0 + QK_ROPE:]], axis=1).astype(BF16)
    wq = w_q_b.reshape(Q_LORA, MLA_HEADS, QK_HEAD)
    wqb = jnp.concatenate([wq[..., :QK_NOPE],
                           _rope_slab(wq[..., QK_NOPE:QK_NOPE + half], wq[..., QK_NOPE + half:])],
                          axis=-1).reshape(Q_LORA, MLA_HEADS * MLA_QK_PAD).astype(BF16)
    wkv = w_kv_b.reshape(KV_LORA, MLA_HEADS, QK_NOPE + V_HEAD)
    wkvb = jnp.concatenate([wkv[..., :QK_NOPE].reshape(KV_LORA, -1),
                            wkv[..., QK_NOPE:].reshape(KV_LORA, -1)], axis=1).astype(BF16)

    def gain(g):
        return jnp.concatenate([g[:QK_NOPE], _rope_slab(g[QK_NOPE:QK_NOPE + half],
                                                        g[QK_NOPE + half:])])[None, :]

    gq = gain(q_norm) * (QK_HEAD ** -0.5 * LOG2E)
    gk = gain(k_norm)
    cos, sin = _rope_tables(seq_len, QK_ROPE)
    return win, wqb, wkvb, gq, gk, _rope_slab(cos, cos), _rope_slab(-sin, sin)


def _trunk(x, w):
    b, s, d = x.shape

    def ffn(x, layer, which):
        y = _ffn(x.reshape(b * s, d), w["ffn_norm"], w["wg"], w["wu"], w["wd"], layer, which)
        return y.reshape(b, s, d)

    x = ffn(x, 0, 0)
    win, wqb, wkvb, gq, gk, cos, sin = w["mla"]
    q, k, v, gb, cu = _mla_proj(x, w["mix_norm"][0][None, :], win, w["qag"], w["kvag"], wqb, wkvb,
                                gq, gk, cos, sin)
    attn = _mla_attn(q, k, v)
    x = _mla_out(attn, gb, cu, w["conv_w"], w["ab_wo"], x)
    x = ffn(x, 0, 1)
    x = ffn(x, 1, 0)
    cwin, cgq, cgk, ccos, csin, sink = w["gqa"]
    q, k, v = _gqa_proj(x, w["mix_norm"][1][None, :], cwin, cgq, cgk, ccos, csin)
    o = _win_attn(q, k, v, sink)
    x = _proj_residual(o, w["c_wo"], x)
    x = ffn(x, 1, 1)
    return x


def kernel(x_prompt, x_sample, ffn_norm, ffn_w_gate, ffn_w_up, ffn_w_down, mix_norm, ab_w_in, ab_q_a_norm, ab_w_q_b, ab_kv_a_norm, ab_w_kv_b, ab_q_norm, ab_k_norm, ab_conv_w, ab_w_out, c_w_in, c_q_norm, c_k_norm, c_sink, c_w_out):
    s = x_prompt.shape[1]
    cos, sin = _rope_tables(s, C_HEAD_DIM)
    w = {
        "ffn_norm": ffn_norm[:, :, None, :],
        "wg": ffn_w_gate.astype(BF16),
        "wu": ffn_w_up.astype(BF16),
        "wd": ffn_w_down.astype(BF16),
        "mix_norm": mix_norm,
        "mla": _prep_mla(ab_w_in[0], ab_w_q_b[0], ab_w_kv_b[0], ab_q_norm[0], ab_k_norm[0], s),
        "qag": ab_q_a_norm[0][None, :],
        "kvag": ab_kv_a_norm[0][None, :],
        "conv_w": ab_conv_w[0],
        "ab_wo": ab_w_out[0].astype(BF16),
        "gqa": (c_w_in[0].astype(BF16),
                c_q_norm[0][None, :] * (C_HEAD_DIM ** -0.5 * LOG2E),
                c_k_norm[0][None, :],
                jnp.concatenate([cos, cos], axis=-1),
                jnp.concatenate([-sin, sin], axis=-1),
                jnp.broadcast_to((c_sink[0] * LOG2E).reshape(C_KV_HEADS, C_GROUP, 1),
                                 (C_KV_HEADS, C_GROUP, LANES))),
        "c_wo": c_w_out[0].astype(BF16),
    }
    return _trunk(x_prompt, w), _trunk(x_sample, w)
```

```python
import functools

import jax
import jax.numpy as jnp
from jax import lax
from jax.experimental import pallas as pl
from jax.experimental.pallas import tpu as pltpu

D_MODEL = 2048
D_FF = 5632
MLA_HEADS = 8
Q_LORA = 512
KV_LORA = 512
QK_NOPE = 128
QK_ROPE = 64
QK_HEAD = QK_NOPE + QK_ROPE
V_HEAD = 128
MLA_WIDTH = MLA_HEADS * V_HEAD
CONV_WIDTH = D_MODEL - MLA_WIDTH
C_HEADS = 16
C_KV_HEADS = 4
C_GROUP = C_HEADS // C_KV_HEADS
C_HEAD_DIM = 128
WINDOW = 128
ROPE_THETA = 10000.0
EPS = 1e-6
NEG = -1e30
LOG2E = 1.4426950408889634

LANES = 128
MLA_QK_PAD = 2 * LANES
AB_IN_PAD = Q_LORA + KV_LORA + 3 * CONV_WIDTH + LANES
VMEM_LIMIT = 56 * 1024 * 1024

F32 = jnp.float32
BF16 = jnp.bfloat16


def _params(*sem, vmem=VMEM_LIMIT):
    return pltpu.CompilerParams(dimension_semantics=sem, vmem_limit_bytes=vmem)


def _resident(shape):
    nd = len(shape)
    return pl.BlockSpec(shape, lambda *_: (0,) * nd, pipeline_mode=pl.Buffered(1))


def _rms(x, gain):
    ms = jnp.mean(x * x, axis=-1, keepdims=True)
    return x * lax.rsqrt(ms + EPS) * gain


def _ffn_body(x_hbm, g_ref, wgu_ref, wd_ref, o_ref, xbuf, hbuf, sem, *, chunk):
    i = pl.program_id(0)
    j = pl.program_id(1)
    n = pl.num_programs(0)
    tm = xbuf.shape[1]
    slot = i % 2
    nslot = 1 - slot
    gain = g_ref[...]

    def x_copy(tile, s):
        return pltpu.make_async_copy(x_hbm.at[pl.ds(tile * tm, tm), :], xbuf.at[s], sem.at[s])

    @pl.when((i == 0) & (j == 0))
    def _():
        x_copy(0, 0).start()
        x_copy(1, 1).start()
        x_copy(0, 0).wait()
        hbuf[0] = _rms(xbuf[0], gain).astype(BF16)
        x_copy(1, 1).wait()

    @pl.when((i >= 1) & (j == 0) & (i + 1 < n))
    def _():
        x_copy(i + 1, nslot).wait()

    @pl.when((j == 1) & (i + 2 < n))
    def _():
        x_copy(i + 2, slot).start()

    def step(first):
        h = hbuf[slot]
        half = wgu_ref.shape[1] // 4

        def act(part):
            z = jnp.dot(h, wgu_ref[:, 2 * part * half:2 * (part + 1) * half],
                        preferred_element_type=F32)
            gate, up = z[:, :half], z[:, half:]
            return (gate * jax.nn.sigmoid(gate) * up * 0.5).astype(BF16)

        a = jnp.concatenate([act(0), act(1)], axis=1)
        y = jnp.dot(a, wd_ref[...], preferred_element_type=F32)
        if first:
            o_ref[...] = xbuf[slot] + y
        else:
            o_ref[...] += y
        rows = pl.ds(pl.multiple_of(jnp.minimum(j * chunk, tm - chunk), 16), chunk)
        hbuf[nslot, rows, :] = _rms(xbuf[nslot, rows, :], gain).astype(BF16)

    pl.when(j == 0)(functools.partial(step, True))
    pl.when(j > 0)(functools.partial(step, False))


FFN_TM = 1024
FFN_TF = 512
FFN_VMEM_LIMIT = 62 * 1024 * 1024
BF16_SUBLANES = 16


def _gate_up_tiles(wg, wu, tf=None):
    tf = tf or FFN_TF
    *lead, d, f = wg.shape
    parts = [w.reshape(*lead, d, f // tf, 2, 1, tf // 2) for w in (wg, wu)]
    return jnp.concatenate(parts, axis=-2).reshape(*lead, d, 2 * f)


def _ffn(x, gain, wgu, wd, layer, which, *, tf=FFN_TF):
    t, d = x.shape
    f = wd.shape[-2]
    tm = min(FFN_TM, t // 2)
    nf = f // tf
    chunk = -(-pl.cdiv(tm, nf) // BF16_SUBLANES) * BF16_SUBLANES
    assert t % tm == 0 and f % tf == 0 and chunk <= tm and tm % BF16_SUBLANES == 0
    return pl.pallas_call(
        functools.partial(_ffn_body, chunk=chunk),
        grid=(t // tm, nf),
        in_specs=[
            pl.BlockSpec(memory_space=pl.ANY),
            pl.BlockSpec((None, None, 1, d), lambda i, j: (layer, which, 0, 0)),
            pl.BlockSpec((None, None, d, 2 * tf), lambda i, j: (layer, which, 0, j)),
            pl.BlockSpec((None, None, tf, d), lambda i, j: (layer, which, j, 0)),
        ],
        out_specs=pl.BlockSpec((tm, d), lambda i, j: (i, 0)),
        out_shape=jax.ShapeDtypeStruct((t, d), F32),
        scratch_shapes=[pltpu.VMEM((2, tm, d), F32), pltpu.VMEM((2, tm, d), BF16),
                        pltpu.SemaphoreType.DMA((2,))],
        compiler_params=_params("arbitrary", "arbitrary", vmem=FFN_VMEM_LIMIT),
        name="ffn",
    )(x, gain, wgu, wd)


def _mla_proj_body(x_ref, g_ref, win_ref, qag_ref, kvag_ref, wqb_ref, wkvb_ref, gq_ref, gk_ref,
                   cos_ref, sin_ref, q_ref, k_ref, v_ref, gb_ref, cu_ref):
    h = _rms(x_ref[...], g_ref[...]).astype(BF16)
    c0 = Q_LORA + KV_LORA
    c1 = c0 + LANES
    z = jnp.dot(h, win_ref[:, :c1], preferred_element_type=F32)
    k_rope = z[:, c0:c1]

    qa = _rms(z[:, :Q_LORA], qag_ref[...]).astype(BF16)
    kva = _rms(z[:, Q_LORA:c0], kvag_ref[...]).astype(BF16)
    q = jnp.dot(qa, wqb_ref[...], preferred_element_type=F32)
    kv = jnp.dot(kva, wkvb_ref[...], preferred_element_type=F32)
    cos = cos_ref[...]
    sin = sin_ref[...]
    gq = gq_ref[...]
    gk = gk_ref[...]

    def rope(t):
        return t * cos + pltpu.roll(t, LANES // 2, 1) * sin

    kr_rot = rope(k_rope * gk[:, LANES:])
    kr_ss = jnp.sum(k_rope * k_rope, axis=-1, keepdims=True)
    for hd in range(MLA_HEADS):
        lo = hd * MLA_QK_PAD
        qn = q[:, lo:lo + LANES]
        qr = q[:, lo + LANES:lo + MLA_QK_PAD]
        ss = jnp.sum(qn * qn, axis=-1, keepdims=True) + jnp.sum(qr * qr, axis=-1, keepdims=True)
        r = lax.rsqrt(ss * (1.0 / QK_HEAD) + EPS)
        q_ref[hd, :, :LANES] = (qn * r * gq[:, :LANES]).astype(BF16)
        q_ref[hd, :, LANES:] = rope(qr * r * gq[:, LANES:]).astype(BF16)
        kn = kv[:, hd * QK_NOPE:(hd + 1) * QK_NOPE]
        ssk = jnp.sum(kn * kn, axis=-1, keepdims=True) + kr_ss
        rk = lax.rsqrt(ssk * (1.0 / QK_HEAD) + EPS)
        k_ref[hd, :, :LANES] = (kn * rk * gk[:, :LANES]).astype(BF16)
        k_ref[hd, :, LANES:] = (kr_rot * rk).astype(BF16)
        v0 = MLA_HEADS * QK_NOPE + hd * V_HEAD
        v_ref[hd] = kv[:, v0:v0 + V_HEAD].astype(BF16)

    zc = jnp.dot(h, win_ref[:, c1:], preferred_element_type=F32)
    gb_ref[...] = zc[:, :CONV_WIDTH].astype(BF16)
    cu_ref[...] = (zc[:, CONV_WIDTH:2 * CONV_WIDTH] * zc[:, 2 * CONV_WIDTH:]).astype(BF16)


def _mla_proj(x, gain, win, qag, kvag, wqb, wkvb, gq, gk, cos, sin, *, tm=256):
    b, s, d = x.shape
    tok = lambda w: pl.BlockSpec((None, tm, w), lambda bi, i: (bi, i, 0))
    bf = lambda w: jax.ShapeDtypeStruct((b, s, w), BF16)
    head = lambda w: pl.BlockSpec((None, MLA_HEADS, tm, w), lambda bi, i: (bi, 0, i, 0))
    hbf = lambda w: jax.ShapeDtypeStruct((b, MLA_HEADS, s, w), BF16)
    return pl.pallas_call(
        _mla_proj_body,
        grid=(b, s // tm),
        in_specs=[
            tok(d),
            _resident((1, d)),
            _resident(win.shape),
            _resident((1, Q_LORA)),
            _resident((1, KV_LORA)),
            _resident(wqb.shape),
            _resident(wkvb.shape),
            _resident((1, MLA_QK_PAD)),
            _resident((1, MLA_QK_PAD)),
            pl.BlockSpec((tm, LANES), lambda bi, i: (i, 0)),
            pl.BlockSpec((tm, LANES), lambda bi, i: (i, 0)),
        ],
        out_specs=[head(MLA_QK_PAD), head(MLA_QK_PAD), head(V_HEAD), tok(CONV_WIDTH), tok(CONV_WIDTH)],
        out_shape=[hbf(MLA_QK_PAD), hbf(MLA_QK_PAD), hbf(V_HEAD), bf(CONV_WIDTH), bf(CONV_WIDTH)],
        compiler_params=_params("parallel", "parallel"),
        name="mla_proj",
    )(x, gain, win, qag, kvag, wqb, wkvb, gq, gk, cos, sin)


MLA_TQ = 256


def _mla_attn_body(q_ref, k_ref, v_ref, o_ref, sa_ref, sb_ref, v1_ref):
    n_tiles = q_ref.shape[0] // MLA_TQ

    def scores(t):
        q = q_ref[pl.ds(pl.multiple_of(t * MLA_TQ, MLA_TQ), MLA_TQ), :]
        return lax.dot_general(q, k_ref[...], (((1,), (1,)), ((), ())), preferred_element_type=F32)

    def finish(s_ref, t):
        s = s_ref[...]
        p = jnp.exp2(s - jnp.max(s, axis=-1, keepdims=True)).astype(BF16)
        o = jnp.dot(p, v1_ref[...], preferred_element_type=F32)
        rows = pl.ds(pl.multiple_of(t * MLA_TQ, MLA_TQ), MLA_TQ)
        o_ref[rows, :] = (o[:, :V_HEAD] / o[:, V_HEAD:]).astype(BF16)

    v1_ref[:, :V_HEAD] = v_ref[...]
    v1_ref[:, V_HEAD:] = jnp.ones((v1_ref.shape[0], V_HEAD), BF16)
    sa_ref[...] = scores(0)

    def pair(j, carry):
        sb_ref[...] = scores(2 * j + 1)
        finish(sa_ref, 2 * j)
        sa_ref[...] = scores(2 * j + 2)
        finish(sb_ref, 2 * j + 1)
        return carry

    lax.fori_loop(0, n_tiles // 2 - 1, pair, 0)
    sb_ref[...] = scores(n_tiles - 1)
    finish(sa_ref, n_tiles - 2)
    finish(sb_ref, n_tiles - 1)


def _mla_attn(q, k, v):
    b, h, s, _ = q.shape
    head = lambda w: pl.BlockSpec((None, None, s, w), lambda bi, hi: (bi, hi, 0, 0))
    return pl.pallas_call(
        _mla_attn_body,
        grid=(b, h),
        in_specs=[head(MLA_QK_PAD), head(MLA_QK_PAD), head(V_HEAD)],
        out_specs=pl.BlockSpec((None, s, V_HEAD), lambda bi, hi: (bi, 0, hi)),
        out_shape=jax.ShapeDtypeStruct((b, s, h * V_HEAD), BF16),
        scratch_shapes=[pltpu.VMEM((MLA_TQ, s), F32), pltpu.VMEM((MLA_TQ, s), F32),
                        pltpu.VMEM((s, 2 * V_HEAD), BF16)],
        compiler_params=_params("parallel", "parallel"),
        name="mla_attn",
    )(q, k, v)


CONV_HALO = 16


def _mla_out_body(attn_ref, gb_ref, cu_ref, prev_ref, next_ref, cw_ref, wo_ref, x_ref, o_ref):
    i = pl.program_id(1)
    tm = cu_ref.shape[0]
    o_ref[...] = x_ref[...] + jnp.dot(attn_ref[...], wo_ref[:MLA_WIDTH, :],
                                      preferred_element_type=F32)
    cu = cu_ref[...].astype(F32)
    prev_row = prev_ref[...].astype(F32)[CONV_HALO - 1:CONV_HALO, :] * (i > 0).astype(F32)
    next_row = next_ref[...].astype(F32)[0:1, :] * (i < pl.num_programs(1) - 1).astype(F32)
    row = lax.broadcasted_iota(jnp.int32, (tm, 1), 0)
    before = jnp.where(row == 0, prev_row, pltpu.roll(cu, 1, 0))
    after = jnp.where(row == tm - 1, next_row, pltpu.roll(cu, tm - 1, 0))
    cw = cw_ref[...]
    conv = before * cw[0:1, :] + cu * cw[1:2, :] + after * cw[2:3, :]
    conv_out = (gb_ref[...].astype(F32) * conv).astype(BF16)
    o_ref[...] += jnp.dot(conv_out, wo_ref[MLA_WIDTH:, :], preferred_element_type=F32)


def _mla_out(attn, gb, cu, conv_w, wo, x, *, tm=512):
    b, s, d = x.shape
    hb = tm // CONV_HALO
    last = s // CONV_HALO - 1
    tok = lambda w: pl.BlockSpec((None, tm, w), lambda bi, i: (bi, i, 0))
    return pl.pallas_call(
        _mla_out_body,
        grid=(b, s // tm),
        in_specs=[
            tok(MLA_WIDTH), tok(CONV_WIDTH), tok(CONV_WIDTH),
            pl.BlockSpec((None, CONV_HALO, CONV_WIDTH),
                         lambda bi, i: (bi, jnp.maximum(i * hb - 1, 0), 0)),
            pl.BlockSpec((None, CONV_HALO, CONV_WIDTH),
                         lambda bi, i: (bi, jnp.minimum((i + 1) * hb, last), 0)),
            _resident(conv_w.shape),
            _resident(wo.shape),
            tok(d),
        ],
        out_specs=tok(d),
        out_shape=jax.ShapeDtypeStruct((b, s, d), F32),
        compiler_params=_params("parallel", "parallel"),
        name="mla_out",
    )(attn, gb, cu, cu, cu, conv_w, wo, x)


def _gqa_proj_body(x_ref, g_ref, win_ref, gq_ref, gk_ref, cos_ref, sin_ref, q_ref, k_ref, v_ref):
    h = _rms(x_ref[...], g_ref[...]).astype(BF16)
    cos = cos_ref[...]
    sin = sin_ref[...]
    gq = gq_ref[...]
    gk = gk_ref[...]

    def norm_rope(t, gain):
        t = _rms(t, gain)
        return (t * cos + pltpu.roll(t, C_HEAD_DIM // 2, 1) * sin).astype(BF16)

    def heads(group):
        lo = group * C_GROUP * C_HEAD_DIM
        z = jnp.dot(h, win_ref[:, lo:lo + C_GROUP * C_HEAD_DIM], preferred_element_type=F32)
        return [z[:, i * C_HEAD_DIM:(i + 1) * C_HEAD_DIM] for i in range(C_GROUP)]

    for group in range(C_KV_HEADS):
        for i, t in enumerate(heads(group)):
            q_ref[group * C_GROUP + i] = norm_rope(t, gq)
    for hd, t in enumerate(heads(C_KV_HEADS)):
        k_ref[hd] = norm_rope(t, gk)
    for hd, t in enumerate(heads(C_KV_HEADS + 1)):
        v_ref[hd] = t.astype(BF16)


def _gqa_proj(x, gain, win, gq, gk, cos, sin, *, tm=256):
    b, s, d = x.shape
    tok = lambda w: pl.BlockSpec((None, tm, w), lambda bi, i: (bi, i, 0))
    head = lambda n: pl.BlockSpec((None, n, tm, C_HEAD_DIM), lambda bi, i: (bi, 0, i, 0))
    hbf = lambda n: jax.ShapeDtypeStruct((b, n, s, C_HEAD_DIM), BF16)
    return pl.pallas_call(
        _gqa_proj_body,
        grid=(b, s // tm),
        in_specs=[
            tok(d),
            _resident((1, d)),
            _resident(win.shape),
            _resident((1, C_HEAD_DIM)),
            _resident((1, C_HEAD_DIM)),
            pl.BlockSpec((tm, C_HEAD_DIM), lambda bi, i: (i, 0)),
            pl.BlockSpec((tm, C_HEAD_DIM), lambda bi, i: (i, 0)),
        ],
        out_specs=[head(C_HEADS), head(C_KV_HEADS), head(C_KV_HEADS)],
        out_shape=[hbf(C_HEADS), hbf(C_KV_HEADS), hbf(C_KV_HEADS)],
        compiler_params=_params("parallel", "parallel"),
        name="gqa_proj",
    )(x, gain, win, gq, gk, cos, sin)


WIN_TQ = 256
WIN_KW = WIN_TQ + 2 * WINDOW


def _win_attn_body(q_ref, k_ref, v_ref, sink_ref, o_ref, sa_ref, sb_ref, v1_ref, lim_ref):
    s_len = k_ref.shape[0]
    n_tiles = s_len // WIN_TQ
    sink = sink_ref[...][:, 0:1].reshape(C_GROUP, 1, 1)

    def key_start(t):
        return pl.multiple_of(jnp.clip(t * WIN_TQ - WINDOW, 0, s_len - WIN_KW), WINDOW)

    def scores(t):
        rows = pl.ds(pl.multiple_of(t * WIN_TQ, WIN_TQ), WIN_TQ)
        q = q_ref[:, rows, :].reshape(C_GROUP * WIN_TQ, C_HEAD_DIM)
        kk = k_ref[pl.ds(key_start(t), WIN_KW), :]
        return lax.dot_general(q, kk, (((1,), (1,)), ((), ())), preferred_element_type=F32)

    def finish(s_ref, t):
        start = key_start(t)
        lim = lim_ref[(t * WIN_TQ - start) // WINDOW]
        s = jnp.minimum(s_ref[...].reshape(C_GROUP, WIN_TQ, WIN_KW), lim[None])
        m = jnp.maximum(jnp.max(s, axis=-1, keepdims=True), sink)
        p = jnp.exp2(s - m).astype(BF16).reshape(C_GROUP * WIN_TQ, WIN_KW)
        o = jnp.dot(p, v1_ref[pl.ds(start, WIN_KW), :], preferred_element_type=F32)
        l = o[:, C_HEAD_DIM:] + jnp.exp2(sink - m).reshape(C_GROUP * WIN_TQ, 1)
        out = (o[:, :C_HEAD_DIM] / l).astype(BF16)
        rows = pl.ds(pl.multiple_of(t * WIN_TQ, WIN_TQ), WIN_TQ)
        for g in range(C_GROUP):
            o_ref[rows, g * C_HEAD_DIM:(g + 1) * C_HEAD_DIM] = out[g * WIN_TQ:(g + 1) * WIN_TQ]

    rel = (lax.broadcasted_iota(jnp.int32, (WIN_TQ, 1), 0)
           - lax.broadcasted_iota(jnp.int32, (1, WIN_KW), 1))
    for k in range(3):
        lim_ref[k] = jnp.where(jnp.abs(rel + k * WINDOW) <= WINDOW, jnp.inf, NEG).astype(F32)
    v1_ref[:, :C_HEAD_DIM] = v_ref[...]
    v1_ref[:, C_HEAD_DIM:] = jnp.ones((s_len, C_HEAD_DIM), BF16)
    sa_ref[...] = scores(0)

    def pair(j, carry):
        sb_ref[...] = scores(2 * j + 1)
        finish(sa_ref, 2 * j)
        sa_ref[...] = scores(2 * j + 2)
        finish(sb_ref, 2 * j + 1)
        return carry

    lax.fori_loop(0, n_tiles // 2 - 1, pair, 0)
    sb_ref[...] = scores(n_tiles - 1)
    finish(sa_ref, n_tiles - 2)
    finish(sb_ref, n_tiles - 1)


def _win_attn(q, k, v, sink):
    b, _, s, _ = q.shape
    kv = pl.BlockSpec((None, None, s, C_HEAD_DIM), lambda bi, h: (bi, h, 0, 0))
    return pl.pallas_call(
        _win_attn_body,
        grid=(b, C_KV_HEADS),
        in_specs=[
            pl.BlockSpec((None, C_GROUP, s, C_HEAD_DIM), lambda bi, h: (bi, h, 0, 0)),
            kv, kv,
            pl.BlockSpec((None, C_GROUP, LANES), lambda bi, h: (h, 0, 0)),
        ],
        out_specs=pl.BlockSpec((None, s, C_GROUP * C_HEAD_DIM), lambda bi, h: (bi, 0, h)),
        out_shape=jax.ShapeDtypeStruct((b, s, C_HEADS * C_HEAD_DIM), BF16),
        scratch_shapes=[pltpu.VMEM((C_GROUP * WIN_TQ, WIN_KW), F32),
                        pltpu.VMEM((C_GROUP * WIN_TQ, WIN_KW), F32),
                        pltpu.VMEM((s, 2 * C_HEAD_DIM), BF16),
                        pltpu.VMEM((3, WIN_TQ, WIN_KW), F32)],
        compiler_params=_params("parallel", "parallel"),
        name="win_attn",
    )(q, k, v, sink)


def _proj_residual_body(a_ref, w_ref, x_ref, o_ref):
    o_ref[...] = x_ref[...] + jnp.dot(a_ref[...], w_ref[...], preferred_element_type=F32)


def _proj_residual(a, w, x, *, tm=512):
    b, s, d = x.shape
    tok = lambda width: pl.BlockSpec((None, tm, width), lambda bi, i: (bi, i, 0))
    return pl.pallas_call(
        _proj_residual_body,
        grid=(b, s // tm),
        in_specs=[tok(a.shape[-1]), _resident(w.shape), tok(d)],
        out_specs=tok(d),
        out_shape=jax.ShapeDtypeStruct((b, s, d), F32),
        compiler_params=_params("parallel", "parallel"),
        name="proj_residual",
    )(a, w, x)


def _rope_tables(seq_len, dim):
    inv = 1.0 / (ROPE_THETA ** (jnp.arange(0, dim, 2, dtype=F32) / dim))
    ang = jnp.arange(seq_len, dtype=F32)[:, None] * inv[None, :]
    return jnp.cos(ang), jnp.sin(ang)


def _rope_slab(x1, x2):
    z = jnp.zeros_like(x1)
    return jnp.concatenate([x1, z, x2, z], axis=-1)


def _prep_mla(w_in, w_q_b, w_kv_b, q_norm, k_norm, seq_len):
    half = QK_ROPE // 2
    c0 = Q_LORA + KV_LORA
    kr = w_in[:, c0:c0 + QK_ROPE]
    win = jnp.concatenate([w_in[:, :c0], _rope_slab(kr[:, :half], kr[:, half:]),
                           w_in[:, c0 + QK_ROPE:]], axis=1).astype(BF16)
    wq = w_q_b.reshape(Q_LORA, MLA_HEADS, QK_HEAD)
    wqb = jnp.concatenate([wq[..., :QK_NOPE],
                           _rope_slab(wq[..., QK_NOPE:QK_NOPE + half], wq[..., QK_NOPE + half:])],
                          axis=-1).reshape(Q_LORA, MLA_HEADS * MLA_QK_PAD).astype(BF16)
    wkv = w_kv_b.reshape(KV_LORA, MLA_HEADS, QK_NOPE + V_HEAD)
    wkvb = jnp.concatenate([wkv[..., :QK_NOPE].reshape(KV_LORA, -1),
                            wkv[..., QK_NOPE:].reshape(KV_LORA, -1)], axis=1).astype(BF16)

    def gain(g):
        return jnp.concatenate([g[:QK_NOPE], _rope_slab(g[QK_NOPE:QK_NOPE + half],
                                                        g[QK_NOPE + half:])])[None, :]

    gq = gain(q_norm) * (QK_HEAD ** -0.5 * LOG2E)
    gk = gain(k_norm)
    cos, sin = _rope_tables(seq_len, QK_ROPE)
    return win, wqb, wkvb, gq, gk, _rope_slab(cos, cos), _rope_slab(-sin, sin)


def _trunk(x, w):
    b, s, d = x.shape

    def ffn(x, layer, which):
        y = _ffn(x.reshape(b * s, d), w["ffn_norm"], w["wgu"], w["wd"], layer, which)
        return y.reshape(b, s, d)

    x = ffn(x, 0, 0)
    win, wqb, wkvb, gq, gk, cos, sin = w["mla"]
    q, k, v, gb, cu = _mla_proj(x, w["mix_norm"][0][None, :], win, w["qag"], w["kvag"], wqb, wkvb,
                                gq, gk, cos, sin)
    attn = _mla_attn(q, k, v)
    x = _mla_out(attn, gb, cu, w["conv_w"], w["ab_wo"], x)
    x = ffn(x, 0, 1)
    x = ffn(x, 1, 0)
    cwin, cgq, cgk, ccos, csin, sink = w["gqa"]
    q, k, v = _gqa_proj(x, w["mix_norm"][1][None, :], cwin, cgq, cgk, ccos, csin)
    o = _win_attn(q, k, v, sink)
    x = _proj_residual(o, w["c_wo"], x)
    x = ffn(x, 1, 1)
    return x


def kernel(x_prompt, x_sample, ffn_norm, ffn_w_gate, ffn_w_up, ffn_w_down, mix_norm, ab_w_in, ab_q_a_norm, ab_w_q_b, ab_kv_a_norm, ab_w_kv_b, ab_q_norm, ab_k_norm, ab_conv_w, ab_w_out, c_w_in, c_q_norm, c_k_norm, c_sink, c_w_out):
    s = x_prompt.shape[1]
    cos, sin = _rope_tables(s, C_HEAD_DIM)
    w = {
        "ffn_norm": ffn_norm[:, :, None, :],
        "wgu": _gate_up_tiles(ffn_w_gate.astype(BF16), ffn_w_up.astype(BF16)),
        "wd": ffn_w_down.astype(BF16),
        "mix_norm": mix_norm,
        "mla": _prep_mla(ab_w_in[0], ab_w_q_b[0], ab_w_kv_b[0], ab_q_norm[0], ab_k_norm[0], s),
        "qag": ab_q_a_norm[0][None, :],
        "kvag": ab_kv_a_norm[0][None, :],
        "conv_w": ab_conv_w[0],
        "ab_wo": ab_w_out[0].astype(BF16),
        "gqa": (c_w_in[0].astype(BF16),
                c_q_norm[0][None, :] * (C_HEAD_DIM ** -0.5 * LOG2E),
                c_k_norm[0][None, :],
                jnp.concatenate([cos, cos], axis=-1),
                jnp.concatenate([-sin, sin], axis=-1),
                jnp.broadcast_to((c_sink[0] * LOG2E).reshape(C_KV_HEADS, C_GROUP, 1),
                                 (C_KV_HEADS, C_GROUP, LANES))),
        "c_wo": c_w_out[0].astype(BF16),
    }
    return _trunk(x_prompt, w), _trunk(x_sample, w)
```

```python
import functools

import jax
import jax.numpy as jnp
from jax import lax
from jax.experimental import pallas as pl
from jax.experimental.pallas import tpu as pltpu

D_MODEL = 2048
MLA_HEADS = 8
Q_LORA = 512
KV_LORA = 512
QK_NOPE = 128
QK_ROPE = 64
QK_HEAD = QK_NOPE + QK_ROPE
V_HEAD = 128
MLA_WIDTH = MLA_HEADS * V_HEAD
CONV_WIDTH = D_MODEL - MLA_WIDTH
C_HEADS = 16
C_KV_HEADS = 4
C_GROUP = C_HEADS // C_KV_HEADS
C_HEAD_DIM = 128
WINDOW = 128
ROPE_THETA = 10000.0
EPS = 1e-6
NEG = -1e30
LOG2E = 1.4426950408889634

LANES = 128
MLA_QK_PAD = 2 * LANES
VMEM_LIMIT = 56 * 1024 * 1024

F32 = jnp.float32
BF16 = jnp.bfloat16


def _params(*sem, vmem=VMEM_LIMIT):
    return pltpu.CompilerParams(dimension_semantics=sem, vmem_limit_bytes=vmem)


def _resident(shape):
    nd = len(shape)
    return pl.BlockSpec(shape, lambda *_: (0,) * nd, pipeline_mode=pl.Buffered(1))


def _rms(x, gain):
    ms = jnp.mean(x * x, axis=-1, keepdims=True)
    return x * lax.rsqrt(ms + EPS) * gain


def _ffn_body(x_hbm, g_ref, wg_ref, wu_ref, wd_ref, o_ref, xbuf, hbuf, sem, *, chunk):
    i = pl.program_id(0)
    j = pl.program_id(1)
    n = pl.num_programs(0)
    tm = xbuf.shape[1]
    slot = i % 2
    nslot = 1 - slot
    gain = g_ref[...]

    def x_copy(tile, s):
        return pltpu.make_async_copy(x_hbm.at[pl.ds(tile * tm, tm), :], xbuf.at[s], sem.at[s])

    @pl.when((i == 0) & (j == 0))
    def _():
        x_copy(0, 0).start()
        x_copy(1, 1).start()
        x_copy(0, 0).wait()
        hbuf[0] = _rms(xbuf[0], gain).astype(BF16)
        x_copy(1, 1).wait()

    @pl.when((i >= 1) & (j == 0) & (i + 1 < n))
    def _():
        x_copy(i + 1, nslot).wait()

    @pl.when((j == 1) & (i + 2 < n))
    def _():
        x_copy(i + 2, slot).start()

    def step(first):
        h = hbuf[slot]
        half = wg_ref.shape[1] // 2

        def act(cols):
            gate = jnp.dot(h, wg_ref[:, cols], preferred_element_type=F32)
            up = jnp.dot(h, wu_ref[:, cols], preferred_element_type=F32)
            return (gate * jax.nn.sigmoid(gate) * up * 0.5).astype(BF16)

        a = jnp.concatenate([act(slice(0, half)), act(slice(half, 2 * half))], axis=1)
        y = jnp.dot(a, wd_ref[...], preferred_element_type=F32)
        if first:
            o_ref[...] = xbuf[slot] + y
        else:
            o_ref[...] += y
        rows = pl.ds(pl.multiple_of(jnp.minimum(j * chunk, tm - chunk), 16), chunk)
        hbuf[nslot, rows, :] = _rms(xbuf[nslot, rows, :], gain).astype(BF16)

    pl.when(j == 0)(functools.partial(step, True))
    pl.when(j > 0)(functools.partial(step, False))


FFN_TM = 1024
FFN_TF = 512
FFN_VMEM_LIMIT = 62 * 1024 * 1024
BF16_SUBLANES = 16


def _ffn(x, gain, wg, wu, wd, layer, which, *, tf=FFN_TF):
    t, d = x.shape
    f = wg.shape[-1]
    tm = min(FFN_TM, t // 2)
    nf = f // tf
    chunk = -(-pl.cdiv(tm, nf) // BF16_SUBLANES) * BF16_SUBLANES
    assert t % tm == 0 and f % tf == 0 and chunk <= tm and tm % BF16_SUBLANES == 0
    return pl.pallas_call(
        functools.partial(_ffn_body, chunk=chunk),
        grid=(t // tm, nf),
        in_specs=[
            pl.BlockSpec(memory_space=pl.ANY),
            pl.BlockSpec((None, None, 1, d), lambda i, j: (layer, which, 0, 0)),
            pl.BlockSpec((None, None, d, tf), lambda i, j: (layer, which, 0, j)),
            pl.BlockSpec((None, None, d, tf), lambda i, j: (layer, which, 0, j)),
            pl.BlockSpec((None, None, tf, d), lambda i, j: (layer, which, j, 0)),
        ],
        out_specs=pl.BlockSpec((tm, d), lambda i, j: (i, 0)),
        out_shape=jax.ShapeDtypeStruct((t, d), F32),
        scratch_shapes=[pltpu.VMEM((2, tm, d), F32), pltpu.VMEM((2, tm, d), BF16),
                        pltpu.SemaphoreType.DMA((2,))],
        compiler_params=_params("arbitrary", "arbitrary", vmem=FFN_VMEM_LIMIT),
        name="ffn",
    )(x, gain, wg, wu, wd)


def _mla_proj_body(x_ref, g_ref, win_ref, qag_ref, kvag_ref, wqb_ref, wkvb_ref, gq_ref, gk_ref,
                   cos_ref, sin_ref, q_ref, k_ref, v_ref, gb_ref, cu_ref):
    h = _rms(x_ref[...], g_ref[...]).astype(BF16)
    c0 = Q_LORA + KV_LORA
    c1 = c0 + LANES
    z = jnp.dot(h, win_ref[:, :c1], preferred_element_type=F32)
    k_rope = z[:, c0:c1]

    qa = _rms(z[:, :Q_LORA], qag_ref[...]).astype(BF16)
    kva = _rms(z[:, Q_LORA:c0], kvag_ref[...]).astype(BF16)
    q = jnp.dot(qa, wqb_ref[...], preferred_element_type=F32)
    kv = jnp.dot(kva, wkvb_ref[...], preferred_element_type=F32)
    cos = cos_ref[...]
    sin = sin_ref[...]
    gq = gq_ref[...]
    gk = gk_ref[...]

    def rope(t):
        return t * cos + pltpu.roll(t, LANES // 2, 1) * sin

    kr_rot = rope(k_rope * gk[:, LANES:])
    kr_ss = jnp.sum(k_rope * k_rope, axis=-1, keepdims=True)
    for hd in range(MLA_HEADS):
        lo = hd * MLA_QK_PAD
        qn = q[:, lo:lo + LANES]
        qr = q[:, lo + LANES:lo + MLA_QK_PAD]
        ss = jnp.sum(qn * qn, axis=-1, keepdims=True) + jnp.sum(qr * qr, axis=-1, keepdims=True)
        r = lax.rsqrt(ss * (1.0 / QK_HEAD) + EPS)
        q_ref[hd, :, :LANES] = (qn * r * gq[:, :LANES]).astype(BF16)
        q_ref[hd, :, LANES:] = rope(qr * r * gq[:, LANES:]).astype(BF16)
        kn = kv[:, hd * QK_NOPE:(hd + 1) * QK_NOPE]
        ssk = jnp.sum(kn * kn, axis=-1, keepdims=True) + kr_ss
        rk = lax.rsqrt(ssk * (1.0 / QK_HEAD) + EPS)
        k_ref[hd, :, :LANES] = (kn * rk * gk[:, :LANES]).astype(BF16)
        k_ref[hd, :, LANES:] = (kr_rot * rk).astype(BF16)
        v0 = MLA_HEADS * QK_NOPE + hd * V_HEAD
        v_ref[hd] = kv[:, v0:v0 + V_HEAD].astype(BF16)

    zc = jnp.dot(h, win_ref[:, c1:], preferred_element_type=F32)
    gb_ref[...] = zc[:, :CONV_WIDTH].astype(BF16)
    cu_ref[...] = (zc[:, CONV_WIDTH:2 * CONV_WIDTH] * zc[:, 2 * CONV_WIDTH:]).astype(BF16)


def _mla_proj(x, gain, win, qag, kvag, wqb, wkvb, gq, gk, cos, sin, *, tm=256):
    b, s, d = x.shape
    tok = lambda w: pl.BlockSpec((None, tm, w), lambda bi, i: (bi, i, 0))
    bf = lambda w: jax.ShapeDtypeStruct((b, s, w), BF16)
    head = lambda w: pl.BlockSpec((None, MLA_HEADS, tm, w), lambda bi, i: (bi, 0, i, 0))
    hbf = lambda w: jax.ShapeDtypeStruct((b, MLA_HEADS, s, w), BF16)
    return pl.pallas_call(
        _mla_proj_body,
        grid=(b, s // tm),
        in_specs=[
            tok(d),
            _resident((1, d)),
            _resident(win.shape),
            _resident((1, Q_LORA)),
            _resident((1, KV_LORA)),
            _resident(wqb.shape),
            _resident(wkvb.shape),
            _resident((1, MLA_QK_PAD)),
            _resident((1, MLA_QK_PAD)),
            pl.BlockSpec((tm, LANES), lambda bi, i: (i, 0)),
            pl.BlockSpec((tm, LANES), lambda bi, i: (i, 0)),
        ],
        out_specs=[head(MLA_QK_PAD), head(MLA_QK_PAD), head(V_HEAD), tok(CONV_WIDTH), tok(CONV_WIDTH)],
        out_shape=[hbf(MLA_QK_PAD), hbf(MLA_QK_PAD), hbf(V_HEAD), bf(CONV_WIDTH), bf(CONV_WIDTH)],
        compiler_params=_params("parallel", "parallel"),
        name="mla_proj",
    )(x, gain, win, qag, kvag, wqb, wkvb, gq, gk, cos, sin)


MLA_TQ = 256


def _mla_attn_body(q_ref, k_ref, v_ref, o_ref, sa_ref, sb_ref, v1_ref):
    n_tiles = q_ref.shape[0] // MLA_TQ

    def scores(t):
        q = q_ref[pl.ds(pl.multiple_of(t * MLA_TQ, MLA_TQ), MLA_TQ), :]
        return lax.dot_general(q, k_ref[...], (((1,), (1,)), ((), ())), preferred_element_type=F32)

    def finish(s_ref, t):
        s = s_ref[...]
        p = jnp.exp2(s - jnp.max(s, axis=-1, keepdims=True)).astype(BF16)
        o = jnp.dot(p, v1_ref[...], preferred_element_type=F32)
        rows = pl.ds(pl.multiple_of(t * MLA_TQ, MLA_TQ), MLA_TQ)
        o_ref[rows, :] = (o[:, :V_HEAD] / o[:, V_HEAD:]).astype(BF16)

    v1_ref[:, :V_HEAD] = v_ref[...]
    v1_ref[:, V_HEAD:] = jnp.ones((v1_ref.shape[0], V_HEAD), BF16)
    sa_ref[...] = scores(0)

    def pair(j, carry):
        sb_ref[...] = scores(2 * j + 1)
        finish(sa_ref, 2 * j)
        sa_ref[...] = scores(2 * j + 2)
        finish(sb_ref, 2 * j + 1)
        return carry

    lax.fori_loop(0, n_tiles // 2 - 1, pair, 0, unroll=True)
    sb_ref[...] = scores(n_tiles - 1)
    finish(sa_ref, n_tiles - 2)
    finish(sb_ref, n_tiles - 1)


def _mla_attn(q, k, v):
    b, h, s, _ = q.shape
    head = lambda w: pl.BlockSpec((None, None, s, w), lambda bi, hi: (bi, hi, 0, 0))
    return pl.pallas_call(
        _mla_attn_body,
        grid=(b, h),
        in_specs=[head(MLA_QK_PAD), head(MLA_QK_PAD), head(V_HEAD)],
        out_specs=pl.BlockSpec((None, s, V_HEAD), lambda bi, hi: (bi, 0, hi)),
        out_shape=jax.ShapeDtypeStruct((b, s, h * V_HEAD), BF16),
        scratch_shapes=[pltpu.VMEM((MLA_TQ, s), F32), pltpu.VMEM((MLA_TQ, s), F32),
                        pltpu.VMEM((s, 2 * V_HEAD), BF16)],
        compiler_params=_params("parallel", "parallel"),
        name="mla_attn",
    )(q, k, v)


CONV_HALO = 16


def _mla_out_body(attn_ref, gb_ref, cu_ref, prev_ref, next_ref, cw_ref, wo_ref, x_ref, o_ref):
    i = pl.program_id(1)
    tm = cu_ref.shape[0]
    cu = cu_ref[...].astype(F32)
    prev_row = prev_ref[...].astype(F32)[CONV_HALO - 1:CONV_HALO, :] * (i > 0).astype(F32)
    next_row = next_ref[...].astype(F32)[0:1, :] * (i < pl.num_programs(1) - 1).astype(F32)
    row = lax.broadcasted_iota(jnp.int32, (tm, 1), 0)
    before = jnp.where(row == 0, prev_row, pltpu.roll(cu, 1, 0))
    after = jnp.where(row == tm - 1, next_row, pltpu.roll(cu, tm - 1, 0))
    cw = cw_ref[...]
    conv = before * cw[0:1, :] + cu * cw[1:2, :] + after * cw[2:3, :]
    conv_out = (gb_ref[...].astype(F32) * conv).astype(BF16)
    y = jnp.dot(attn_ref[...], wo_ref[:MLA_WIDTH, :], preferred_element_type=F32)
    y += jnp.dot(conv_out, wo_ref[MLA_WIDTH:, :], preferred_element_type=F32)
    o_ref[...] = x_ref[...] + y


def _mla_out(attn, gb, cu, conv_w, wo, x, *, tm=512):
    b, s, d = x.shape
    hb = tm // CONV_HALO
    last = s // CONV_HALO - 1
    tok = lambda w: pl.BlockSpec((None, tm, w), lambda bi, i: (bi, i, 0))
    return pl.pallas_call(
        _mla_out_body,
        grid=(b, s // tm),
        in_specs=[
            tok(MLA_WIDTH), tok(CONV_WIDTH), tok(CONV_WIDTH),
            pl.BlockSpec((None, CONV_HALO, CONV_WIDTH),
                         lambda bi, i: (bi, jnp.maximum(i * hb - 1, 0), 0)),
            pl.BlockSpec((None, CONV_HALO, CONV_WIDTH),
                         lambda bi, i: (bi, jnp.minimum((i + 1) * hb, last), 0)),
            _resident(conv_w.shape),
            _resident(wo.shape),
            tok(d),
        ],
        out_specs=tok(d),
        out_shape=jax.ShapeDtypeStruct((b, s, d), F32),
        compiler_params=_params("parallel", "parallel"),
        name="mla_out",
    )(attn, gb, cu, cu, cu, conv_w, wo, x)


def _gqa_proj_body(x_ref, g_ref, win_ref, gq_ref, gk_ref, cos_ref, sin_ref, q_ref, k_ref, v_ref):
    h = _rms(x_ref[...], g_ref[...]).astype(BF16)
    cos = cos_ref[...]
    sin = sin_ref[...]
    gq = gq_ref[...]
    gk = gk_ref[...]

    def norm_rope(t, gain):
        t = _rms(t, gain)
        return (t * cos + pltpu.roll(t, C_HEAD_DIM // 2, 1) * sin).astype(BF16)

    def heads(group):
        lo = group * C_GROUP * C_HEAD_DIM
        z = jnp.dot(h, win_ref[:, lo:lo + C_GROUP * C_HEAD_DIM], preferred_element_type=F32)
        return [z[:, i * C_HEAD_DIM:(i + 1) * C_HEAD_DIM] for i in range(C_GROUP)]

    for group in range(C_KV_HEADS):
        for i, t in enumerate(heads(group)):
            q_ref[group * C_GROUP + i] = norm_rope(t, gq)
    for hd, t in enumerate(heads(C_KV_HEADS)):
        k_ref[hd] = norm_rope(t, gk)
    for hd, t in enumerate(heads(C_KV_HEADS + 1)):
        v_ref[hd] = t.astype(BF16)


def _gqa_proj(x, gain, win, gq, gk, cos, sin, *, tm=256):
    b, s, d = x.shape
    tok = lambda w: pl.BlockSpec((None, tm, w), lambda bi, i: (bi, i, 0))
    head = lambda n: pl.BlockSpec((None, n, tm, C_HEAD_DIM), lambda bi, i: (bi, 0, i, 0))
    hbf = lambda n: jax.ShapeDtypeStruct((b, n, s, C_HEAD_DIM), BF16)
    return pl.pallas_call(
        _gqa_proj_body,
        grid=(b, s // tm),
        in_specs=[
            tok(d),
            _resident((1, d)),
            _resident(win.shape),
            _resident((1, C_HEAD_DIM)),
            _resident((1, C_HEAD_DIM)),
            pl.BlockSpec((tm, C_HEAD_DIM), lambda bi, i: (i, 0)),
            pl.BlockSpec((tm, C_HEAD_DIM), lambda bi, i: (i, 0)),
        ],
        out_specs=[head(C_HEADS), head(C_KV_HEADS), head(C_KV_HEADS)],
        out_shape=[hbf(C_HEADS), hbf(C_KV_HEADS), hbf(C_KV_HEADS)],
        compiler_params=_params("parallel", "parallel"),
        name="gqa_proj",
    )(x, gain, win, gq, gk, cos, sin)


WIN_TQ = 256
WIN_KW = WIN_TQ + 2 * WINDOW


def _win_attn_body(q_ref, k_ref, v_ref, sink_ref, o_ref, sa_ref, sb_ref, v1_ref, lim_ref):
    s_len = k_ref.shape[0]
    n_tiles = s_len // WIN_TQ
    sink = sink_ref[...][:, 0:1].reshape(C_GROUP, 1, 1)

    def key_start(t):
        return pl.multiple_of(jnp.clip(t * WIN_TQ - WINDOW, 0, s_len - WIN_KW), WINDOW)

    def scores(t):
        rows = pl.ds(pl.multiple_of(t * WIN_TQ, WIN_TQ), WIN_TQ)
        q = q_ref[:, rows, :].reshape(C_GROUP * WIN_TQ, C_HEAD_DIM)
        kk = k_ref[pl.ds(key_start(t), WIN_KW), :]
        return lax.dot_general(q, kk, (((1,), (1,)), ((), ())), preferred_element_type=F32)

    def finish(s_ref, t):
        start = key_start(t)
        lim = lim_ref[(t * WIN_TQ - start) // WINDOW]
        s = jnp.minimum(s_ref[...].reshape(C_GROUP, WIN_TQ, WIN_KW), lim[None])
        m = jnp.maximum(jnp.max(s, axis=-1, keepdims=True), sink)
        p = jnp.exp2(s - m).astype(BF16).reshape(C_GROUP * WIN_TQ, WIN_KW)
        o = jnp.dot(p, v1_ref[pl.ds(start, WIN_KW), :], preferred_element_type=F32)
        l = o[:, C_HEAD_DIM:] + jnp.exp2(sink - m).reshape(C_GROUP * WIN_TQ, 1)
        out = (o[:, :C_HEAD_DIM] / l).astype(BF16)
        rows = pl.ds(pl.multiple_of(t * WIN_TQ, WIN_TQ), WIN_TQ)
        for g in range(C_GROUP):
            o_ref[rows, g * C_HEAD_DIM:(g + 1) * C_HEAD_DIM] = out[g * WIN_TQ:(g + 1) * WIN_TQ]

    rel = (lax.broadcasted_iota(jnp.int32, (WIN_TQ, 1), 0)
           - lax.broadcasted_iota(jnp.int32, (1, WIN_KW), 1))
    for k in range(3):
        lim_ref[k] = jnp.where(jnp.abs(rel + k * WINDOW) <= WINDOW, jnp.inf, NEG).astype(F32)
    v1_ref[:, :C_HEAD_DIM] = v_ref[...]
    v1_ref[:, C_HEAD_DIM:] = jnp.ones((s_len, C_HEAD_DIM), BF16)
    sa_ref[...] = scores(0)

    def pair(j, carry):
        sb_ref[...] = scores(2 * j + 1)
        finish(sa_ref, 2 * j)
        sa_ref[...] = scores(2 * j + 2)
        finish(sb_ref, 2 * j + 1)
        return carry

    lax.fori_loop(0, n_tiles // 2 - 1, pair, 0)
    sb_ref[...] = scores(n_tiles - 1)
    finish(sa_ref, n_tiles - 2)
    finish(sb_ref, n_tiles - 1)


def _win_attn(q, k, v, sink):
    b, _, s, _ = q.shape
    kv = pl.BlockSpec((None, None, s, C_HEAD_DIM), lambda bi, h: (bi, h, 0, 0))
    return pl.pallas_call(
        _win_attn_body,
        grid=(b, C_KV_HEADS),
        in_specs=[
            pl.BlockSpec((None, C_GROUP, s, C_HEAD_DIM), lambda bi, h: (bi, h, 0, 0)),
            kv, kv,
            pl.BlockSpec((None, C_GROUP, LANES), lambda bi, h: (h, 0, 0)),
        ],
        out_specs=pl.BlockSpec((None, s, C_GROUP * C_HEAD_DIM), lambda bi, h: (bi, 0, h)),
        out_shape=jax.ShapeDtypeStruct((b, s, C_HEADS * C_HEAD_DIM), BF16),
        scratch_shapes=[pltpu.VMEM((C_GROUP * WIN_TQ, WIN_KW), F32),
                        pltpu.VMEM((C_GROUP * WIN_TQ, WIN_KW), F32),
                        pltpu.VMEM((s, 2 * C_HEAD_DIM), BF16),
                        pltpu.VMEM((3, WIN_TQ, WIN_KW), F32)],
        compiler_params=_params("parallel", "parallel"),
        name="win_attn",
    )(q, k, v, sink)


def _proj_residual_body(a_ref, w_ref, x_ref, o_ref):
    o_ref[...] = x_ref[...] + jnp.dot(a_ref[...], w_ref[...], preferred_element_type=F32)


def _proj_residual(a, w, x, *, tm=512):
    b, s, d = x.shape
    tok = lambda width: pl.BlockSpec((None, tm, width), lambda bi, i: (bi, i, 0))
    return pl.pallas_call(
        _proj_residual_body,
        grid=(b, s // tm),
        in_specs=[tok(a.shape[-1]), _resident(w.shape), tok(d)],
        out_specs=tok(d),
        out_shape=jax.ShapeDtypeStruct((b, s, d), F32),
        compiler_params=_params("parallel", "parallel"),
        name="proj_residual",
    )(a, w, x)


def _rope_tables(seq_len, dim):
    inv = 1.0 / (ROPE_THETA ** (jnp.arange(0, dim, 2, dtype=F32) / dim))
    ang = jnp.arange(seq_len, dtype=F32)[:, None] * inv[None, :]
    return jnp.cos(ang), jnp.sin(ang)


def _rope_slab(x1, x2):
    z = jnp.zeros_like(x1)
    return jnp.concatenate([x1, z, x2, z], axis=-1)


def _prep_mla(w_in, w_q_b, w_kv_b, q_norm, k_norm, seq_len):
    half = QK_ROPE // 2
    c0 = Q_LORA + KV_LORA
    kr = w_in[:, c0:c0 + QK_ROPE]
    win = jnp.concatenate([w_in[:, :c0], _rope_slab(kr[:, :half], kr[:, half:]),
                           w_in[:, c0 + QK_ROPE:]], axis=1).astype(BF16)
    wq = w_q_b.reshape(Q_LORA, MLA_HEADS, QK_HEAD)
    wqb = jnp.concatenate([wq[..., :QK_NOPE],
                           _rope_slab(wq[..., QK_NOPE:QK_NOPE + half], wq[..., QK_NOPE + half:])],
                          axis=-1).reshape(Q_LORA, MLA_HEADS * MLA_QK_PAD).astype(BF16)
    wkv = w_kv_b.reshape(KV_LORA, MLA_HEADS, QK_NOPE + V_HEAD)
    wkvb = jnp.concatenate([wkv[..., :QK_NOPE].reshape(KV_LORA, -1),
                            wkv[..., QK_NOPE:].reshape(KV_LORA, -1)], axis=1).astype(BF16)

    def gain(g):
        return jnp.concatenate([g[:QK_NOPE], _rope_slab(g[QK_NOPE:QK_NOPE + half],
                                                        g[QK_NOPE + half:])])[None, :]

    gq = gain(q_norm) * (QK_HEAD ** -0.5 * LOG2E)
    gk = gain(k_norm)
    cos, sin = _rope_tables(seq_len, QK_ROPE)
    return win, wqb, wkvb, gq, gk, _rope_slab(cos, cos), _rope_slab(-sin, sin)


def _trunk(x, w):
    b, s, d = x.shape

    def ffn(x, layer, which):
        y = _ffn(x.reshape(b * s, d), w["ffn_norm"], w["wg"], w["wu"], w["wd"], layer, which)
        return y.reshape(b, s, d)

    x = ffn(x, 0, 0)
    win, wqb, wkvb, gq, gk, cos, sin = w["mla"]
    q, k, v, gb, cu = _mla_proj(x, w["mix_norm"][0][None, :], win, w["qag"], w["kvag"], wqb, wkvb,
                                gq, gk, cos, sin)
    attn = _mla_attn(q, k, v)
    x = _mla_out(attn, gb, cu, w["conv_w"], w["ab_wo"], x)
    x = ffn(x, 0, 1)
    x = ffn(x, 1, 0)
    cwin, cgq, cgk, ccos, csin, sink = w["gqa"]
    q, k, v = _gqa_proj(x, w["mix_norm"][1][None, :], cwin, cgq, cgk, ccos, csin)
    o = _win_attn(q, k, v, sink)
    x = _proj_residual(o, w["c_wo"], x)
    x = ffn(x, 1, 1)
    return x


def kernel(x_prompt, x_sample, ffn_norm, ffn_w_gate, ffn_w_up, ffn_w_down, mix_norm, ab_w_in, ab_q_a_norm, ab_w_q_b, ab_kv_a_norm, ab_w_kv_b, ab_q_norm, ab_k_norm, ab_conv_w, ab_w_out, c_w_in, c_q_norm, c_k_norm, c_sink, c_w_out):
    s = x_prompt.shape[1]
    cos, sin = _rope_tables(s, C_HEAD_DIM)
    w = {
        "ffn_norm": ffn_norm[:, :, None, :],
        "wg": ffn_w_gate.astype(BF16),
        "wu": ffn_w_up.astype(BF16),
        "wd": ffn_w_down.astype(BF16),
        "mix_norm": mix_norm,
        "mla": _prep_mla(ab_w_in[0], ab_w_q_b[0], ab_w_kv_b[0], ab_q_norm[0], ab_k_norm[0], s),
        "qag": ab_q_a_norm[0][None, :],
        "kvag": ab_kv_a_norm[0][None, :],
        "conv_w": ab_conv_w[0],
        "ab_wo": ab_w_out[0].astype(BF16),
        "gqa": (c_w_in[0].astype(BF16),
                c_q_norm[0][None, :] * (C_HEAD_DIM ** -0.5 * LOG2E),
                c_k_norm[0][None, :],
                jnp.concatenate([cos, cos], axis=-1),
                jnp.concatenate([-sin, sin], axis=-1),
                jnp.broadcast_to((c_sink[0] * LOG2E).reshape(C_KV_HEADS, C_GROUP, 1),
                                 (C_KV_HEADS, C_GROUP, LANES))),
        "c_wo": c_w_out[0].astype(BF16),
    }
    return _trunk(x_prompt, w), _trunk(x_sample, w)
```

```python
import functools

import jax
import jax.numpy as jnp
from jax import lax
from jax.experimental import pallas as pl
from jax.experimental.pallas import tpu as pltpu

D_MODEL = 2048
MLA_HEADS = 8
Q_LORA = 512
KV_LORA = 512
QK_NOPE = 128
QK_ROPE = 64
QK_HEAD = QK_NOPE + QK_ROPE
V_HEAD = 128
MLA_WIDTH = MLA_HEADS * V_HEAD
CONV_WIDTH = D_MODEL - MLA_WIDTH
C_HEADS = 16
C_KV_HEADS = 4
C_GROUP = C_HEADS // C_KV_HEADS
C_HEAD_DIM = 128
WINDOW = 128
ROPE_THETA = 10000.0
EPS = 1e-6
NEG = -1e30
LOG2E = 1.4426950408889634

LANES = 128
MLA_QK_PAD = 2 * LANES
VMEM_LIMIT = 56 * 1024 * 1024

F32 = jnp.float32
BF16 = jnp.bfloat16


def _params(*sem, vmem=VMEM_LIMIT):
    return pltpu.CompilerParams(dimension_semantics=sem, vmem_limit_bytes=vmem)


def _resident(shape):
    nd = len(shape)
    return pl.BlockSpec(shape, lambda *_: (0,) * nd, pipeline_mode=pl.Buffered(1))


def _rms(x, gain):
    ms = jnp.mean(x * x, axis=-1, keepdims=True)
    return x * lax.rsqrt(ms + EPS) * gain


def _ffn_body(x_hbm, g_ref, wg_ref, wu_ref, wd_ref, o_ref, xbuf, hbuf, sem, *, chunk):
    i = pl.program_id(0)
    j = pl.program_id(1)
    n = pl.num_programs(0)
    tm = xbuf.shape[1]
    slot = i % 2
    nslot = 1 - slot
    gain = g_ref[...]

    def x_copy(tile, s):
        return pltpu.make_async_copy(x_hbm.at[pl.ds(tile * tm, tm), :], xbuf.at[s], sem.at[s])

    @pl.when((i == 0) & (j == 0))
    def _():
        x_copy(0, 0).start()
        x_copy(1, 1).start()
        x_copy(0, 0).wait()
        hbuf[0] = _rms(xbuf[0], gain).astype(BF16)
        x_copy(1, 1).wait()

    @pl.when((i >= 1) & (j == 0) & (i + 1 < n))
    def _():
        x_copy(i + 1, nslot).wait()

    @pl.when((j == 1) & (i + 2 < n))
    def _():
        x_copy(i + 2, slot).start()

    def step(first):
        h = hbuf[slot]
        half = wg_ref.shape[1] // 2

        def act(cols):
            gate = jnp.dot(h, wg_ref[:, cols], preferred_element_type=F32)
            up = jnp.dot(h, wu_ref[:, cols], preferred_element_type=F32)
            return (gate * jax.nn.sigmoid(gate) * up * 0.5).astype(BF16)

        a = jnp.concatenate([act(slice(0, half)), act(slice(half, 2 * half))], axis=1)
        y = jnp.dot(a, wd_ref[...], preferred_element_type=F32)
        if first:
            o_ref[...] = xbuf[slot] + y
        else:
            o_ref[...] += y
        rows = pl.ds(pl.multiple_of(jnp.minimum(j * chunk, tm - chunk), 16), chunk)
        hbuf[nslot, rows, :] = _rms(xbuf[nslot, rows, :], gain).astype(BF16)

    pl.when(j == 0)(functools.partial(step, True))
    pl.when(j > 0)(functools.partial(step, False))


FFN_TM = 1024
FFN_TF = 512
FFN_VMEM_LIMIT = 62 * 1024 * 1024
BF16_SUBLANES = 16


def _ffn(x, gain, wg, wu, wd, layer, which, *, tf=FFN_TF):
    t, d = x.shape
    f = wg.shape[-1]
    tm = min(FFN_TM, t // 2)
    nf = f // tf
    chunk = -(-pl.cdiv(tm, nf) // BF16_SUBLANES) * BF16_SUBLANES
    assert t % tm == 0 and f % tf == 0 and chunk <= tm and tm % BF16_SUBLANES == 0
    return pl.pallas_call(
        functools.partial(_ffn_body, chunk=chunk),
        grid=(t // tm, nf),
        in_specs=[
            pl.BlockSpec(memory_space=pl.ANY),
            pl.BlockSpec((None, None, 1, d), lambda i, j: (layer, which, 0, 0)),
            pl.BlockSpec((None, None, d, tf), lambda i, j: (layer, which, 0, j)),
            pl.BlockSpec((None, None, d, tf), lambda i, j: (layer, which, 0, j)),
            pl.BlockSpec((None, None, tf, d), lambda i, j: (layer, which, j, 0)),
        ],
        out_specs=pl.BlockSpec((tm, d), lambda i, j: (i, 0)),
        out_shape=jax.ShapeDtypeStruct((t, d), F32),
        scratch_shapes=[pltpu.VMEM((2, tm, d), F32), pltpu.VMEM((2, tm, d), BF16),
                        pltpu.SemaphoreType.DMA((2,))],
        compiler_params=_params("arbitrary", "arbitrary", vmem=FFN_VMEM_LIMIT),
        name="ffn",
    )(x, gain, wg, wu, wd)


def _mla_proj_body(x_ref, g_ref, win_ref, qag_ref, kvag_ref, wqb_ref, wkvb_ref, gq_ref, gk_ref,
                   cos_ref, sin_ref, q_ref, k_ref, v_ref, gb_ref, cu_ref):
    h = _rms(x_ref[...], g_ref[...]).astype(BF16)
    c0 = Q_LORA + KV_LORA
    c1 = c0 + LANES
    z = jnp.dot(h, win_ref[:, :c1], preferred_element_type=F32)
    k_rope = z[:, c0:c1]

    qa = _rms(z[:, :Q_LORA], qag_ref[...]).astype(BF16)
    kva = _rms(z[:, Q_LORA:c0], kvag_ref[...]).astype(BF16)
    q = jnp.dot(qa, wqb_ref[...], preferred_element_type=F32)
    kv = jnp.dot(kva, wkvb_ref[...], preferred_element_type=F32)
    cos = cos_ref[...]
    sin = sin_ref[...]
    gq = gq_ref[...]
    gk = gk_ref[...]

    def rope(t):
        return t * cos + pltpu.roll(t, LANES // 2, 1) * sin

    kr_rot = rope(k_rope * gk[:, LANES:])
    kr_ss = jnp.sum(k_rope * k_rope, axis=-1, keepdims=True)
    for hd in range(MLA_HEADS):
        lo = hd * MLA_QK_PAD
        qn = q[:, lo:lo + LANES]
        qr = q[:, lo + LANES:lo + MLA_QK_PAD]
        ss = jnp.sum(qn * qn, axis=-1, keepdims=True) + jnp.sum(qr * qr, axis=-1, keepdims=True)
        r = lax.rsqrt(ss * (1.0 / QK_HEAD) + EPS)
        q_ref[hd, :, :LANES] = (qn * r * gq[:, :LANES]).astype(BF16)
        q_ref[hd, :, LANES:] = rope(qr * r * gq[:, LANES:]).astype(BF16)
        kn = kv[:, hd * QK_NOPE:(hd + 1) * QK_NOPE]
        ssk = jnp.sum(kn * kn, axis=-1, keepdims=True) + kr_ss
        rk = lax.rsqrt(ssk * (1.0 / QK_HEAD) + EPS)
        k_ref[hd, :, :LANES] = (kn * rk * gk[:, :LANES]).astype(BF16)
        k_ref[hd, :, LANES:] = (kr_rot * rk).astype(BF16)
        v0 = MLA_HEADS * QK_NOPE + hd * V_HEAD
        v_ref[hd] = kv[:, v0:v0 + V_HEAD].astype(BF16)

    zc = jnp.dot(h, win_ref[:, c1:], preferred_element_type=F32)
    gb_ref[...] = zc[:, :CONV_WIDTH].astype(BF16)
    cu_ref[...] = (zc[:, CONV_WIDTH:2 * CONV_WIDTH] * zc[:, 2 * CONV_WIDTH:]).astype(BF16)


def _mla_proj(x, gain, win, qag, kvag, wqb, wkvb, gq, gk, cos, sin, *, tm=256):
    b, s, d = x.shape
    tok = lambda w: pl.BlockSpec((None, tm, w), lambda bi, i: (bi, i, 0))
    bf = lambda w: jax.ShapeDtypeStruct((b, s, w), BF16)
    head = lambda w: pl.BlockSpec((None, MLA_HEADS, tm, w), lambda bi, i: (bi, 0, i, 0))
    hbf = lambda w: jax.ShapeDtypeStruct((b, MLA_HEADS, s, w), BF16)
    return pl.pallas_call(
        _mla_proj_body,
        grid=(b, s // tm),
        in_specs=[
            tok(d),
            _resident((1, d)),
            _resident(win.shape),
            _resident((1, Q_LORA)),
            _resident((1, KV_LORA)),
            _resident(wqb.shape),
            _resident(wkvb.shape),
            _resident((1, MLA_QK_PAD)),
            _resident((1, MLA_QK_PAD)),
            pl.BlockSpec((tm, LANES), lambda bi, i: (i, 0)),
            pl.BlockSpec((tm, LANES), lambda bi, i: (i, 0)),
        ],
        out_specs=[head(MLA_QK_PAD), head(MLA_QK_PAD), head(V_HEAD), tok(CONV_WIDTH), tok(CONV_WIDTH)],
        out_shape=[hbf(MLA_QK_PAD), hbf(MLA_QK_PAD), hbf(V_HEAD), bf(CONV_WIDTH), bf(CONV_WIDTH)],
        compiler_params=_params("parallel", "parallel"),
        name="mla_proj",
    )(x, gain, win, qag, kvag, wqb, wkvb, gq, gk, cos, sin)


MLA_TQ = 256


def _mla_attn_body(q_ref, k_ref, v_ref, o_ref, sa_ref, sb_ref, v1_ref):
    n_tiles = q_ref.shape[0] // MLA_TQ

    def scores(t):
        q = q_ref[pl.ds(pl.multiple_of(t * MLA_TQ, MLA_TQ), MLA_TQ), :]
        return lax.dot_general(q, k_ref[...], (((1,), (1,)), ((), ())), preferred_element_type=F32)

    def finish(s_ref, t):
        s = s_ref[...]
        p = jnp.exp2(s - jnp.max(s, axis=-1, keepdims=True)).astype(BF16)
        o = jnp.dot(p, v1_ref[...], preferred_element_type=F32)
        rows = pl.ds(pl.multiple_of(t * MLA_TQ, MLA_TQ), MLA_TQ)
        o_ref[rows, :] = (o[:, :V_HEAD] / o[:, V_HEAD:]).astype(BF16)

    v1_ref[:, :V_HEAD] = v_ref[...]
    v1_ref[:, V_HEAD:] = jnp.ones((v1_ref.shape[0], V_HEAD), BF16)
    sa_ref[...] = scores(0)

    def pair(j, carry):
        sb_ref[...] = scores(2 * j + 1)
        finish(sa_ref, 2 * j)
        sa_ref[...] = scores(2 * j + 2)
        finish(sb_ref, 2 * j + 1)
        return carry

    lax.fori_loop(0, n_tiles // 2 - 1, pair, 0, unroll=True)
    sb_ref[...] = scores(n_tiles - 1)
    finish(sa_ref, n_tiles - 2)
    finish(sb_ref, n_tiles - 1)


def _mla_attn(q, k, v):
    b, h, s, _ = q.shape
    head = lambda w: pl.BlockSpec((None, None, s, w), lambda bi, hi: (bi, hi, 0, 0))
    return pl.pallas_call(
        _mla_attn_body,
        grid=(b, h),
        in_specs=[head(MLA_QK_PAD), head(MLA_QK_PAD), head(V_HEAD)],
        out_specs=pl.BlockSpec((None, s, V_HEAD), lambda bi, hi: (bi, 0, hi)),
        out_shape=jax.ShapeDtypeStruct((b, s, h * V_HEAD), BF16),
        scratch_shapes=[pltpu.VMEM((MLA_TQ, s), F32), pltpu.VMEM((MLA_TQ, s), F32),
                        pltpu.VMEM((s, 2 * V_HEAD), BF16)],
        compiler_params=_params("parallel", "parallel"),
        name="mla_attn",
    )(q, k, v)


CONV_HALO = 16


def _mla_out_body(attn_ref, gb_ref, cu_ref, prev_ref, next_ref, cw_ref, wo_ref, x_ref, o_ref):
    i = pl.program_id(1)
    tm = cu_ref.shape[0]
    cu = cu_ref[...].astype(F32)
    prev_row = prev_ref[...].astype(F32)[CONV_HALO - 1:CONV_HALO, :] * (i > 0).astype(F32)
    next_row = next_ref[...].astype(F32)[0:1, :] * (i < pl.num_programs(1) - 1).astype(F32)
    row = lax.broadcasted_iota(jnp.int32, (tm, 1), 0)
    before = jnp.where(row == 0, prev_row, pltpu.roll(cu, 1, 0))
    after = jnp.where(row == tm - 1, next_row, pltpu.roll(cu, tm - 1, 0))
    cw = cw_ref[...]
    conv = before * cw[0:1, :] + cu * cw[1:2, :] + after * cw[2:3, :]
    conv_out = (gb_ref[...].astype(F32) * conv).astype(BF16)
    y = jnp.dot(attn_ref[...], wo_ref[:MLA_WIDTH, :], preferred_element_type=F32)
    y += jnp.dot(conv_out, wo_ref[MLA_WIDTH:, :], preferred_element_type=F32)
    o_ref[...] = x_ref[...] + y


def _mla_out(attn, gb, cu, conv_w, wo, x, *, tm=512):
    b, s, d = x.shape
    hb = tm // CONV_HALO
    last = s // CONV_HALO - 1
    tok = lambda w: pl.BlockSpec((None, tm, w), lambda bi, i: (bi, i, 0))
    return pl.pallas_call(
        _mla_out_body,
        grid=(b, s // tm),
        in_specs=[
            tok(MLA_WIDTH), tok(CONV_WIDTH), tok(CONV_WIDTH),
            pl.BlockSpec((None, CONV_HALO, CONV_WIDTH),
                         lambda bi, i: (bi, jnp.maximum(i * hb - 1, 0), 0)),
            pl.BlockSpec((None, CONV_HALO, CONV_WIDTH),
                         lambda bi, i: (bi, jnp.minimum((i + 1) * hb, last), 0)),
            _resident(conv_w.shape),
            _resident(wo.shape),
            tok(d),
        ],
        out_specs=tok(d),
        out_shape=jax.ShapeDtypeStruct((b, s, d), F32),
        compiler_params=_params("parallel", "parallel"),
        name="mla_out",
    )(attn, gb, cu, cu, cu, conv_w, wo, x)


def _gqa_proj_body(x_ref, g_ref, win_ref, gq_ref, gk_ref, cos_ref, sin_ref, q_ref, k_ref, v_ref):
    h = _rms(x_ref[...], g_ref[...]).astype(BF16)
    cos = cos_ref[...]
    sin = sin_ref[...]
    gq = gq_ref[...]
    gk = gk_ref[...]

    def norm_rope(t, gain):
        t = _rms(t, gain)
        return (t * cos + pltpu.roll(t, C_HEAD_DIM // 2, 1) * sin).astype(BF16)

    def heads(group):
        lo = group * C_GROUP * C_HEAD_DIM
        z = jnp.dot(h, win_ref[:, lo:lo + C_GROUP * C_HEAD_DIM], preferred_element_type=F32)
        return [z[:, i * C_HEAD_DIM:(i + 1) * C_HEAD_DIM] for i in range(C_GROUP)]

    for group in range(C_KV_HEADS):
        for i, t in enumerate(heads(group)):
            q_ref[group * C_GROUP + i] = norm_rope(t, gq)
    for hd, t in enumerate(heads(C_KV_HEADS)):
        k_ref[hd] = norm_rope(t, gk)
    for hd, t in enumerate(heads(C_KV_HEADS + 1)):
        v_ref[hd] = t.astype(BF16)


def _gqa_proj(x, gain, win, gq, gk, cos, sin, *, tm=256):
    b, s, d = x.shape
    tok = lambda w: pl.BlockSpec((None, tm, w), lambda bi, i: (bi, i, 0))
    head = lambda n: pl.BlockSpec((None, n, tm, C_HEAD_DIM), lambda bi, i: (bi, 0, i, 0))
    hbf = lambda n: jax.ShapeDtypeStruct((b, n, s, C_HEAD_DIM), BF16)
    return pl.pallas_call(
        _gqa_proj_body,
        grid=(b, s // tm),
        in_specs=[
            tok(d),
            _resident((1, d)),
            _resident(win.shape),
            _resident((1, C_HEAD_DIM)),
            _resident((1, C_HEAD_DIM)),
            pl.BlockSpec((tm, C_HEAD_DIM), lambda bi, i: (i, 0)),
            pl.BlockSpec((tm, C_HEAD_DIM), lambda bi, i: (i, 0)),
        ],
        out_specs=[head(C_HEADS), head(C_KV_HEADS), head(C_KV_HEADS)],
        out_shape=[hbf(C_HEADS), hbf(C_KV_HEADS), hbf(C_KV_HEADS)],
        compiler_params=_params("parallel", "parallel"),
        name="gqa_proj",
    )(x, gain, win, gq, gk, cos, sin)


WIN_TQ = 256
WIN_KW = WIN_TQ + 2 * WINDOW


def _win_attn_body(q_ref, k_ref, v_ref, sink_ref, o_ref, sa_ref, sb_ref, v1_ref, lim_ref):
    s_len = k_ref.shape[0]
    n_tiles = s_len // WIN_TQ
    sink = sink_ref[...][:, 0:1].reshape(C_GROUP, 1, 1)

    def key_start(t):
        return pl.multiple_of(jnp.clip(t * WIN_TQ - WINDOW, 0, s_len - WIN_KW), WINDOW)

    def scores(t):
        rows = pl.ds(pl.multiple_of(t * WIN_TQ, WIN_TQ), WIN_TQ)
        q = q_ref[:, rows, :].reshape(C_GROUP * WIN_TQ, C_HEAD_DIM)
        kk = k_ref[pl.ds(key_start(t), WIN_KW), :]
        return lax.dot_general(q, kk, (((1,), (1,)), ((), ())), preferred_element_type=F32)

    def finish(s_ref, t):
        start = key_start(t)
        lim = lim_ref[(t * WIN_TQ - start) // WINDOW]
        s = jnp.minimum(s_ref[...].reshape(C_GROUP, WIN_TQ, WIN_KW), lim[None])
        m = jnp.maximum(jnp.max(s, axis=-1, keepdims=True), sink)
        p = jnp.exp2(s - m).astype(BF16).reshape(C_GROUP * WIN_TQ, WIN_KW)
        o = jnp.dot(p, v1_ref[pl.ds(start, WIN_KW), :], preferred_element_type=F32)
        l = o[:, C_HEAD_DIM:] + jnp.exp2(sink - m).reshape(C_GROUP * WIN_TQ, 1)
        out = (o[:, :C_HEAD_DIM] / l).astype(BF16)
        rows = pl.ds(pl.multiple_of(t * WIN_TQ, WIN_TQ), WIN_TQ)
        for g in range(C_GROUP):
            o_ref[rows, g * C_HEAD_DIM:(g + 1) * C_HEAD_DIM] = out[g * WIN_TQ:(g + 1) * WIN_TQ]

    rel = (lax.broadcasted_iota(jnp.int32, (WIN_TQ, 1), 0)
           - lax.broadcasted_iota(jnp.int32, (1, WIN_KW), 1))
    for k in range(3):
        lim_ref[k] = jnp.where(jnp.abs(rel + k * WINDOW) <= WINDOW, jnp.inf, NEG).astype(F32)
    v1_ref[:, :C_HEAD_DIM] = v_ref[...]
    v1_ref[:, C_HEAD_DIM:] = jnp.ones((s_len, C_HEAD_DIM), BF16)
    sa_ref[...] = scores(0)

    def pair(j, carry):
        sb_ref[...] = scores(2 * j + 1)
        finish(sa_ref, 2 * j)
        sa_ref[...] = scores(2 * j + 2)
        finish(sb_ref, 2 * j + 1)
        return carry

    lax.fori_loop(0, n_tiles // 2 - 1, pair, 0)
    sb_ref[...] = scores(n_tiles - 1)
    finish(sa_ref, n_tiles - 2)
    finish(sb_ref, n_tiles - 1)


def _win_attn(q, k, v, sink):
    b, _, s, _ = q.shape
    kv = pl.BlockSpec((None, None, s, C_HEAD_DIM), lambda bi, h: (bi, h, 0, 0))
    return pl.pallas_call(
        _win_attn_body,
        grid=(b, C_KV_HEADS),
        in_specs=[
            pl.BlockSpec((None, C_GROUP, s, C_HEAD_DIM), lambda bi, h: (bi, h, 0, 0)),
            kv, kv,
            pl.BlockSpec((None, C_GROUP, LANES), lambda bi, h: (h, 0, 0)),
        ],
        out_specs=pl.BlockSpec((None, s, C_GROUP * C_HEAD_DIM), lambda bi, h: (bi, 0, h)),
        out_shape=jax.ShapeDtypeStruct((b, s, C_HEADS * C_HEAD_DIM), BF16),
        scratch_shapes=[pltpu.VMEM((C_GROUP * WIN_TQ, WIN_KW), F32),
                        pltpu.VMEM((C_GROUP * WIN_TQ, WIN_KW), F32),
                        pltpu.VMEM((s, 2 * C_HEAD_DIM), BF16),
                        pltpu.VMEM((3, WIN_TQ, WIN_KW), F32)],
        compiler_params=_params("parallel", "parallel"),
        name="win_attn",
    )(q, k, v, sink)


def _proj_residual_body(a_ref, w_ref, x_ref, o_ref):
    o_ref[...] = x_ref[...] + jnp.dot(a_ref[...], w_ref[...], preferred_element_type=F32)


def _proj_residual(a, w, x, *, tm=1024):
    b, s, d = x.shape
    tok = lambda width: pl.BlockSpec((None, tm, width), lambda bi, i: (bi, i, 0))
    return pl.pallas_call(
        _proj_residual_body,
        grid=(b, s // tm),
        in_specs=[tok(a.shape[-1]), _resident(w.shape), tok(d)],
        out_specs=tok(d),
        out_shape=jax.ShapeDtypeStruct((b, s, d), F32),
        compiler_params=_params("parallel", "parallel"),
        name="proj_residual",
    )(a, w, x)


def _rope_tables(seq_len, dim):
    inv = 1.0 / (ROPE_THETA ** (jnp.arange(0, dim, 2, dtype=F32) / dim))
    ang = jnp.arange(seq_len, dtype=F32)[:, None] * inv[None, :]
    return jnp.cos(ang), jnp.sin(ang)


def _rope_slab(x1, x2):
    z = jnp.zeros_like(x1)
    return jnp.concatenate([x1, z, x2, z], axis=-1)


def _prep_mla(w_in, w_q_b, w_kv_b, q_norm, k_norm, seq_len):
    half = QK_ROPE // 2
    c0 = Q_LORA + KV_LORA
    kr = w_in[:, c0:c0 + QK_ROPE]
    win = jnp.concatenate([w_in[:, :c0], _rope_slab(kr[:, :half], kr[:, half:]),
                           w_in[:, c0 + QK_ROPE:]], axis=1).astype(BF16)
    wq = w_q_b.reshape(Q_LORA, MLA_HEADS, QK_HEAD)
    wqb = jnp.concatenate([wq[..., :QK_NOPE],
                           _rope_slab(wq[..., QK_NOPE:QK_NOPE + half], wq[..., QK_NOPE + half:])],
                          axis=-1).reshape(Q_LORA, MLA_HEADS * MLA_QK_PAD).astype(BF16)
    wkv = w_kv_b.reshape(KV_LORA, MLA_HEADS, QK_NOPE + V_HEAD)
    wkvb = jnp.concatenate([wkv[..., :QK_NOPE].reshape(KV_LORA, -1),
                            wkv[..., QK_NOPE:].reshape(KV_LORA, -1)], axis=1).astype(BF16)

    def gain(g):
        return jnp.concatenate([g[:QK_NOPE], _rope_slab(g[QK_NOPE:QK_NOPE + half],
                                                        g[QK_NOPE + half:])])[None, :]

    gq = gain(q_norm) * (QK_HEAD ** -0.5 * LOG2E)
    gk = gain(k_norm)
    cos, sin = _rope_tables(seq_len, QK_ROPE)
    return win, wqb, wkvb, gq, gk, _rope_slab(cos, cos), _rope_slab(-sin, sin)


def _trunk(x, w):
    b, s, d = x.shape

    def ffn(x, layer, which):
        y = _ffn(x.reshape(b * s, d), w["ffn_norm"], w["wg"], w["wu"], w["wd"], layer, which)
        return y.reshape(b, s, d)

    x = ffn(x, 0, 0)
    win, wqb, wkvb, gq, gk, cos, sin = w["mla"]
    q, k, v, gb, cu = _mla_proj(x, w["mix_norm"][0][None, :], win, w["qag"], w["kvag"], wqb, wkvb,
                                gq, gk, cos, sin)
    attn = _mla_attn(q, k, v)
    x = _mla_out(attn, gb, cu, w["conv_w"], w["ab_wo"], x)
    x = ffn(x, 0, 1)
    x = ffn(x, 1, 0)
    cwin, cgq, cgk, ccos, csin, sink = w["gqa"]
    q, k, v = _gqa_proj(x, w["mix_norm"][1][None, :], cwin, cgq, cgk, ccos, csin)
    o = _win_attn(q, k, v, sink)
    x = _proj_residual(o, w["c_wo"], x)
    x = ffn(x, 1, 1)
    return x


def kernel(x_prompt, x_sample, ffn_norm, ffn_w_gate, ffn_w_up, ffn_w_down, mix_norm, ab_w_in, ab_q_a_norm, ab_w_q_b, ab_kv_a_norm, ab_w_kv_b, ab_q_norm, ab_k_norm, ab_conv_w, ab_w_out, c_w_in, c_q_norm, c_k_norm, c_sink, c_w_out):
    s = x_prompt.shape[1]
    cos, sin = _rope_tables(s, C_HEAD_DIM)
    w = {
        "ffn_norm": ffn_norm[:, :, None, :],
        "wg": ffn_w_gate.astype(BF16),
        "wu": ffn_w_up.astype(BF16),
        "wd": ffn_w_down.astype(BF16),
        "mix_norm": mix_norm,
        "mla": _prep_mla(ab_w_in[0], ab_w_q_b[0], ab_w_kv_b[0], ab_q_norm[0], ab_k_norm[0], s),
        "qag": ab_q_a_norm[0][None, :],
        "kvag": ab_kv_a_norm[0][None, :],
        "conv_w": ab_conv_w[0],
        "ab_wo": ab_w_out[0].astype(BF16),
        "gqa": (c_w_in[0].astype(BF16),
                c_q_norm[0][None, :] * (C_HEAD_DIM ** -0.5 * LOG2E),
                c_k_norm[0][None, :],
                jnp.concatenate([cos, cos], axis=-1),
                jnp.concatenate([-sin, sin], axis=-1),
                jnp.broadcast_to((c_sink[0] * LOG2E).reshape(C_KV_HEADS, C_GROUP, 1),
                                 (C_KV_HEADS, C_GROUP, LANES))),
        "c_wo": c_w_out[0].astype(BF16),
    }
    return _trunk(x_prompt, w), _trunk(x_sample, w)
```
